```python
import math
import jax
import jax.numpy as jnp
from jax import lax
import numpy as np

D_MODEL = 1024
BATCH = 16
SEQ = 2048
DEPTH = 1

GRID_W = 64
CTX_LEN = 256
N_MOD = 9
D_FF = 2816
RMS_EPS = 1e-6

RWKV_HEADS = 8
RWKV_HEAD_DIM = 64
RWKV_WIDTH = RWKV_HEADS * RWKV_HEAD_DIM
DECAY_LORA = 64
AAA_LORA = 64
GATE_LORA = 128
RWKV_COLS = 3 * RWKV_WIDTH + 2 * DECAY_LORA + 2 * AAA_LORA + GATE_LORA
RWKV_SPLITS = [RWKV_WIDTH, 2 * RWKV_WIDTH, 3 * RWKV_WIDTH,
               3 * RWKV_WIDTH + 2 * DECAY_LORA,
               3 * RWKV_WIDTH + 2 * DECAY_LORA + 2 * AAA_LORA]
RWKV_GN_EPS = 64e-5
SHIFT_TAPS = 3

DIFF_HEADS = 4
DIFF_HEAD_DIM = 64
DIFF_V_DIM = 2 * DIFF_HEAD_DIM
DIFF_WIDTH = DIFF_HEADS * DIFF_V_DIM
DIFF_COLS = 3 * DIFF_WIDTH
Q_BLOCK = 128
ROPE_BASE = 10000.0
ROPE_FREQS = DIFF_HEAD_DIM // 4

N_BRANCH = 2
MIX_COLS = RWKV_COLS + DIFF_COLS + N_BRANCH * D_MODEL

kernel_name = "hybrid_rwkv7_diffattn_macaron_dit_layer"


def _rms(x, g):
    xf = x.astype(jnp.float32)
    y = xf * lax.rsqrt(jnp.mean(xf * xf, axis=-1, keepdims=True) + RMS_EPS)
    return (y * g).astype(x.dtype)


def _modulate(h, shift, scale):
    return h * (1.0 + scale) + shift


def _adaln(cond, w, b):
    m = jax.nn.silu(cond) @ w + b
    return m.reshape(m.shape[:-1] + (N_MOD, D_MODEL))


def _swiglu(h, w_in, w_out):
    gate, up = jnp.split(h @ w_in, 2, axis=-1)
    return (jax.nn.silu(gate) * up) @ w_out


def _ffn_half_step(x, m, pre_g, post_g, w_in, w_out):
    h = _modulate(_rms(x, pre_g), m[..., 0, :], m[..., 1, :])
    return x + 0.5 * m[..., 2, :] * _rms(_swiglu(h, w_in, w_out), post_g)


def _short_conv(u, w):
    up = jnp.pad(u, ((0, 0), (1, 1), (0, 0)))
    return up[:, :-2] * w[0] + up[:, 1:-1] * w[1] + up[:, 2:] * w[2]


def _axial_angles(rows):
    row = jnp.repeat(jnp.arange(rows, dtype=jnp.float32), GRID_W)
    col = jnp.tile(jnp.arange(GRID_W, dtype=jnp.float32), rows)
    freqs = ROPE_BASE ** (-jnp.arange(ROPE_FREQS, dtype=jnp.float32) / ROPE_FREQS)
    return row[:, None] * freqs, col[:, None] * freqs


def _rotate(z, ang):
    c = jnp.cos(ang)[None, :, None, None, :]
    s = jnp.sin(ang)[None, :, None, None, :]
    z1, z2 = jnp.split(z, 2, axis=-1)
    return jnp.concatenate([z1 * c - z2 * s, z2 * c + z1 * s], axis=-1)


def _rope_2d(z, ang_row, ang_col):
    zr, zc = jnp.split(z, 2, axis=-1)
    return jnp.concatenate([_rotate(zr, ang_row), _rotate(zc, ang_col)], axis=-1).astype(z.dtype)


def _rwkv_streams(u, shift_w, w0, w2, a0, a2, g2, k_k, k_a):
    u = _short_conv(u, shift_w)
    B, T, _ = u.shape
    r, k, v, wd, ad, gd = jnp.split(u, RWKV_SPLITS, axis=-1)
    wd = wd.reshape(B, T, 2, DECAY_LORA)
    ad = ad.reshape(B, T, 2, AAA_LORA)
    w_logit = (w0 + jnp.einsum("btdr,drc->btdc", jnp.tanh(wd), w2)).astype(jnp.float32)
    decay = jnp.exp(-jnp.exp(-jax.nn.softplus(-w_logit) - 0.5))
    a = jax.nn.sigmoid(a0 + jnp.einsum("btdr,drc->btdc", ad, a2))
    g = jax.nn.sigmoid(gd) @ g2
    kk = (k * k_k).astype(jnp.float32).reshape(B, T, RWKV_HEADS, RWKV_HEAD_DIM)
    kk = kk * lax.rsqrt(jnp.sum(kk * kk, axis=-1, keepdims=True) + 1e-12)
    k_dir = k[:, :, None, :] * (1.0 + (a - 1.0) * k_a)
    heads = lambda t: t.reshape(t.shape[:-1] + (RWKV_HEADS, RWKV_HEAD_DIM))
    return heads(r), heads(v), kk, heads(k_dir), heads(decay), heads(a), g


def _rwkv7_scan(s0, r, w, k, v, kk, a, reverse, collect):
    xs = tuple(jnp.moveaxis(t.astype(jnp.float32), 1, 0) for t in (r, w, k, v, kk, a))

    def step(s, inp):
        r_t, w_t, k_t, v_t, kk_t, a_t = inp
        sa = jnp.einsum("bhvk,bhk->bhv", s, kk_t)
        s = (s * w_t[:, :, None, :] - sa[..., None] * (kk_t * a_t)[:, :, None, :]
             + v_t[..., None] * k_t[:, :, None, :])
        y = jnp.einsum("bhvk,bhk->bhv", s, r_t) if collect else None
        return s, y

    s_fin, ys = lax.scan(step, s0, xs, reverse=reverse)
    return s_fin, (jnp.moveaxis(ys, 0, 1) if collect else None)


def _rwkv_readout(y, r, v, k_dir, g, r_k, ln_g, ln_b):
    B, T = y.shape[:2]
    mu = jnp.mean(y, axis=-1, keepdims=True)
    var = jnp.mean(jnp.square(y - mu), axis=-1, keepdims=True)
    yn = ((y - mu) * lax.rsqrt(var + RWKV_GN_EPS)).reshape(B, T, RWKV_WIDTH) * ln_g + ln_b
    bonus = jnp.sum(r[:, :, None] * k_dir * r_k, axis=(2, 4))[..., None] * v
    return ((yn + bonus.reshape(B, T, RWKV_WIDTH)) * g).astype(g.dtype)


def _rwkv_bidir(streams, s0, collect, r_k, ln_g, ln_b):
    r, v, kk, k_dir, decay, a, g = streams
    finals, ys = [], []
    for d, rev in enumerate((False, True)):
        s_fin, y = _rwkv7_scan(s0[d], r, decay[:, :, d], k_dir[:, :, d], v, kk, a[:, :, d], rev, collect)
        finals.append(s_fin)
        ys.append(y)
    if not collect:
        return finals, None
    return finals, _rwkv_readout(ys[0] + ys[1], r, v, k_dir, g, r_k, ln_g, ln_b)


def _diff_split(u):
    B, T, _ = u.shape
    q, k, v = jnp.split(u, 3, axis=-1)
    return (q.reshape(B, T, DIFF_HEADS, 2, DIFF_HEAD_DIM),
            k.reshape(B, T, DIFF_HEADS, 2, DIFF_HEAD_DIM),
            v.reshape(B, T, DIFF_HEADS, DIFF_V_DIM))


def _diff_attend(q, k, v, lam):
    s = jnp.einsum("bqhcd,bkhcd->bhcqk", q, k).astype(jnp.float32) * DIFF_HEAD_DIM ** -0.5
    p = jax.nn.softmax(s, axis=-1)
    attn = (p[:, :, 0] - lam * p[:, :, 1]).astype(v.dtype)
    return jnp.einsum("bhqk,bkhe->bqhe", attn, v)


def _diff_out(o, subln_g, lam_init):
    B, T = o.shape[:2]
    return (_rms(o, subln_g) * (1.0 - lam_init)).reshape(B, T, DIFF_WIDTH)


def _gated_merge(ya, yb, gate_cols, up_a, up_b, w_out):
    ga, gb = jnp.split(jax.nn.sigmoid(gate_cols), N_BRANCH, axis=-1)
    return (ga * (ya @ up_a) + gb * (yb @ up_b)) @ w_out


def setup_inputs(seed: int = 0) -> dict:
    key = jax.random.key(seed)
    ks = iter(jax.random.split(key, 32))
    f32 = jnp.float32
    L, D = DEPTH, D_MODEL

    def nrm(shape, s):
        return jax.random.normal(next(ks), shape, f32) * s

    side = jax.random.uniform(next(ks), (L, 2, RWKV_COLS), f32, 0.0, 0.5)
    shift_w = jnp.stack([side[:, 0], 1.0 - 0.5 * (side[:, 0] + side[:, 1]), side[:, 1]], axis=1)
    return {
        "x": nrm((BATCH, SEQ, D), 1.0),
        "c": nrm((BATCH, D), 1.0),
        "ctx": nrm((BATCH, CTX_LEN, D), 1.0),
        "c_ctx": nrm((D,), 1.0),
        "ada_w": nrm((L, D, N_MOD * D), 0.5 * D ** -0.5),
        "ada_b": nrm((L, N_MOD * D), 0.02),
        "pre_norm_g": 1.0 + nrm((L, 3, D), 0.1),
        "post_norm_g": 1.0 + nrm((L, 3, D), 0.1),
        "ffn1_w_in": nrm((L, D, 2 * D_FF), D ** -0.5),
        "ffn1_w_out": nrm((L, D_FF, D), D_FF ** -0.5),
        "mix_w_in": nrm((L, D, MIX_COLS), D ** -0.5),
        "rwkv_shift_w": shift_w,
        "rwkv_w0": jax.random.uniform(next(ks), (L, 2, RWKV_WIDTH), f32, -6.0, 1.0),
        "rwkv_w2": nrm((L, 2, DECAY_LORA, RWKV_WIDTH), 0.5 * DECAY_LORA ** -0.5),
        "rwkv_a0": nrm((L, 2, RWKV_WIDTH), 0.3),
        "rwkv_a2": nrm((L, 2, AAA_LORA, RWKV_WIDTH), 0.5 * AAA_LORA ** -0.5),
        "rwkv_g2": nrm((L, GATE_LORA, RWKV_WIDTH), GATE_LORA ** -0.5),
        "rwkv_k_k": 0.85 + nrm((L, RWKV_WIDTH), 0.05),
        "rwkv_k_a": 1.0 + nrm((L, RWKV_WIDTH), 0.05),
        "rwkv_r_k": nrm((L, RWKV_HEADS, RWKV_HEAD_DIM), 0.1),
        "rwkv_ln_g": 1.0 + nrm((L, RWKV_WIDTH), 0.1),
        "rwkv_ln_b": nrm((L, RWKV_WIDTH), 0.02),
        "diff_lambda": nrm((L, 4, DIFF_HEAD_DIM), 0.1),
        "diff_subln_g": 1.0 + nrm((L, DIFF_V_DIM), 0.1),
        "branch_up_a": nrm((L, RWKV_WIDTH, D), RWKV_WIDTH ** -0.5),
        "branch_up_b": nrm((L, DIFF_WIDTH, D), DIFF_WIDTH ** -0.5),
        "mix_w_out": nrm((L, D, D), D ** -0.5),
        "ffn2_w_in": nrm((L, D, 2 * D_FF), D ** -0.5),
        "ffn2_w_out": nrm((L, D_FF, D), D_FF ** -0.5),
    }


def reference(x, c, ctx, c_ctx, ada_w, ada_b, pre_norm_g, post_norm_g,
              ffn1_w_in, ffn1_w_out, mix_w_in, rwkv_shift_w, rwkv_w0, rwkv_w2,
              rwkv_a0, rwkv_a2, rwkv_g2, rwkv_k_k, rwkv_k_a, rwkv_r_k,
              rwkv_ln_g, rwkv_ln_b, diff_lambda, diff_subln_g,
              branch_up_a, branch_up_b, mix_w_out, ffn2_w_in, ffn2_w_out):
    B, T, _ = x.shape
    rows = T // GRID_W
    ang_row, ang_col = _axial_angles(rows)
    zero_state = jnp.zeros((B, RWKV_HEADS, RWKV_HEAD_DIM, RWKV_HEAD_DIM), jnp.float32)
    split_cols = [RWKV_COLS, RWKV_COLS + DIFF_COLS]
    n_blk = T // Q_BLOCK

    for l in range(DEPTH):
        last = l == DEPTH - 1
        lam_init = 0.8 - 0.6 * math.exp(-0.3 * l)
        m_x = _adaln(c, ada_w[l], ada_b[l])[:, None]
        m_c = _adaln(c_ctx, ada_w[l], ada_b[l])[None, None]

        x = _ffn_half_step(x, m_x[..., 0:3, :], pre_norm_g[l, 0], post_norm_g[l, 0],
                           ffn1_w_in[l], ffn1_w_out[l])
        ctx = _ffn_half_step(ctx, m_c[..., 0:3, :], pre_norm_g[l, 0], post_norm_g[l, 0],
                             ffn1_w_in[l], ffn1_w_out[l])

        hx = _modulate(_rms(x, pre_norm_g[l, 1]), m_x[..., 3, :], m_x[..., 4, :])
        hc = _modulate(_rms(ctx, pre_norm_g[l, 1]), m_c[..., 3, :], m_c[..., 4, :])
        rx, dx, gx = jnp.split(hx @ mix_w_in[l], split_cols, axis=-1)
        rc, dc, gc = jnp.split(hc @ mix_w_in[l], split_cols, axis=-1)

        rwkv_p = (rwkv_shift_w[l], rwkv_w0[l], rwkv_w2[l], rwkv_a0[l], rwkv_a2[l],
                  rwkv_g2[l], rwkv_k_k[l], rwkv_k_a[l])
        read_p = (rwkv_r_k[l], rwkv_ln_g[l], rwkv_ln_b[l])
        st_c = _rwkv_streams(rc, *rwkv_p)
        st_x = _rwkv_streams(rx, *rwkv_p)
        ctx_states, ya_c = _rwkv_bidir(st_c, (zero_state, zero_state), not last, *read_p)
        _, ya_x = _rwkv_bidir(st_x, ctx_states, True, *read_p)

        lq1, lk1, lq2, lk2 = diff_lambda[l].astype(jnp.float32)
        lam = jnp.exp(jnp.sum(lq1 * lk1)) - jnp.exp(jnp.sum(lq2 * lk2)) + lam_init
        qc, kc, vc = _diff_split(dc)
        qx, kx, vx = _diff_split(dx)
        qx = _rope_2d(qx, ang_row, ang_col)
        kx = _rope_2d(kx, ang_row, ang_col)
        k_all = jnp.concatenate([kc, kx], axis=1)
        v_all = jnp.concatenate([vc, vx], axis=1)
        q_blk = jnp.moveaxis(qx.reshape(B, n_blk, Q_BLOCK, DIFF_HEADS, 2, DIFF_HEAD_DIM), 1, 0)
        o_x = lax.map(lambda qb: _diff_attend(qb, k_all, v_all, lam), q_blk)
        o_x = jnp.moveaxis(o_x, 0, 1).reshape(B, T, DIFF_HEADS, DIFF_V_DIM)
        yb_x = _diff_out(o_x, diff_subln_g[l], lam_init)

        mix_x = _gated_merge(ya_x, yb_x, gx, branch_up_a[l], branch_up_b[l], mix_w_out[l])
        x = x + m_x[..., 5, :] * _rms(mix_x, post_norm_g[l, 1])

        if not last:
            yb_c = _diff_out(_diff_attend(qc, kc, vc, lam), diff_subln_g[l], lam_init)
            mix_c = _gated_merge(ya_c, yb_c, gc, branch_up_a[l], branch_up_b[l], mix_w_out[l])
            ctx = ctx + m_c[..., 5, :] * _rms(mix_c, post_norm_g[l, 1])
            ctx = _ffn_half_step(ctx, m_c[..., 6:9, :], pre_norm_g[l, 2], post_norm_g[l, 2],
                                 ffn2_w_in[l], ffn2_w_out[l])

        x = _ffn_half_step(x, m_x[..., 6:9, :], pre_norm_g[l, 2], post_norm_g[l, 2],
                           ffn2_w_in[l], ffn2_w_out[l])
    return x
```

```python
import functools
import math

import numpy as np
import jax
import jax.numpy as jnp
from jax import lax
from jax.experimental import pallas as pl
from jax.experimental.pallas import tpu as pltpu

D_MODEL = 1024
SEQ = 2048
CTX_LEN = 256
ALL_LEN = CTX_LEN + SEQ
GRID_W = 64
N_MOD = 9
D_FF = 2816
RMS_EPS = 1e-6

RWKV_HEADS = 8
HEAD_DIM = 64
RWKV_WIDTH = 512
LORA_W = 128
GATE_LORA = 128
RWKV_COLS = 1920
RWKV_GN_EPS = 64e-5

DIFF_HEADS = 4
DIFF_WIDTH = 512
DIFF_V_DIM = 128
ROPE_BASE = 10000.0
ROPE_FREQS = 16
LAM_INIT = 0.8 - 0.6 * math.exp(-0.0)

MIX_COLS = RWKV_COLS + 3 * DIFF_WIDTH + 2 * D_MODEL

TM = 256
CHUNK = 64
GROUP = 4
GW = GROUP * HEAD_DIM
N_CHUNKS = ALL_LEN // CHUNK
CTX_CHUNKS = CTX_LEN // CHUNK
HALO = 8
TQ = 512

F32 = jnp.float32
BF16 = jnp.bfloat16
VMEM_LIMIT = 56 * 1024 * 1024


def _const_spec(shape):
    return pl.BlockSpec(shape, lambda *_: (0,) * len(shape), pipeline_mode=pl.Buffered(1))


def _sigmoid(x):
    return 1.0 / (1.0 + jnp.exp(-x))


def _rms(x, g):
    return x * lax.rsqrt(jnp.mean(x * x, axis=-1, keepdims=True) + RMS_EPS) * g


def _dot(a, b):
    return jnp.dot(a.astype(BF16), b.astype(BF16), preferred_element_type=F32)


def _dot_nt(a, b):
    return lax.dot_general(a.astype(BF16), b.astype(BF16), (((1,), (1,)), ((), ())),
                           preferred_element_type=F32)


def _split_dot(x, m, terms, m_left=False):
    acc = None
    rem = x
    for _ in range(terms):
        part = rem.astype(BF16)
        d = jnp.dot(*((m, part) if m_left else (part, m)), preferred_element_type=F32)
        acc = d if acc is None else acc + d
        rem = rem - part.astype(F32)
    return acc


def _ada_kernel(cond_ref, w_ref, b_ref, o_ref):
    c = cond_ref[...]
    o_ref[...] = _dot(c * _sigmoid(c), w_ref[...]) + b_ref[...]


def _adaln(cond, w, b):
    rows = cond.shape[0]
    n = w.shape[1]
    tn = 1152
    return pl.pallas_call(
        _ada_kernel,
        grid=(n // tn,),
        in_specs=[pl.BlockSpec((rows, D_MODEL), lambda i: (0, 0)),
                  pl.BlockSpec((D_MODEL, tn), lambda i: (0, i)),
                  pl.BlockSpec((1, tn), lambda i: (0, i))],
        out_specs=pl.BlockSpec((rows, tn), lambda i: (0, i)),
        out_shape=jax.ShapeDtypeStruct((rows, n), F32),
        name="adaln",
    )(cond, w, b.reshape(1, n))


def _ffn_core(x, mod, pre_g, post_g, w_in_ref, w_out_ref):
    h = (_rms(x, pre_g) * (1.0 + mod[1:2]) + mod[0:1]).astype(BF16)
    gu = jnp.dot(h, w_in_ref[...], preferred_element_type=F32)
    gate, up = gu[:, :D_FF], gu[:, D_FF:]
    act = (gate * _sigmoid(gate) * up).astype(BF16)
    o = jnp.dot(act, w_out_ref[...], preferred_element_type=F32)
    return x + 0.5 * mod[2:3] * _rms(o, post_g)


def _ffn1_kernel(ctx_ref, x_ref, mod_ref, pre_ref, post_ref, w_in_ref, w_out_ref, o_ref):
    is_ctx = pl.program_id(1) == 0
    x = jnp.where(is_ctx, ctx_ref[0], x_ref[0])
    o_ref[0] = _ffn_core(x, mod_ref[0, 0, 0:3], pre_ref[...], post_ref[...], w_in_ref, w_out_ref)


def _ffn1(ctx, x, mods, pre_g, post_g, w_in, w_out):
    b = x.shape[0]
    nt = ALL_LEN // TM
    return pl.pallas_call(
        _ffn1_kernel,
        grid=(b, nt),
        in_specs=[pl.BlockSpec((1, TM, D_MODEL), lambda i, j: (i, 0, 0)),
                  pl.BlockSpec((1, TM, D_MODEL), lambda i, j: (i, jnp.maximum(j - 1, 0), 0)),
                  pl.BlockSpec((1, 1, N_MOD, D_MODEL), lambda i, j: (i, jnp.minimum(j, 1), 0, 0)),
                  _const_spec((1, D_MODEL)), _const_spec((1, D_MODEL)),
                  _const_spec((D_MODEL, 2 * D_FF)), _const_spec((D_FF, D_MODEL))],
        out_specs=pl.BlockSpec((1, TM, D_MODEL), lambda i, j: (i, j, 0)),
        out_shape=jax.ShapeDtypeStruct((b, ALL_LEN, D_MODEL), F32),
        compiler_params=pltpu.CompilerParams(
            dimension_semantics=("arbitrary", "arbitrary"), vmem_limit_bytes=VMEM_LIMIT),
        name="ffn1",
    )(ctx, x, mods, pre_g, post_g, w_in, w_out)


def _ffn2_kernel(x_ref, mod_ref, pre_ref, post_ref, w_in_ref, w_out_ref, o_ref):
    o_ref[0] = _ffn_core(x_ref[0], mod_ref[0, 0, 6:9], pre_ref[...], post_ref[...], w_in_ref, w_out_ref)


def _ffn2(x, mods, pre_g, post_g, w_in, w_out):
    b = x.shape[0]
    return pl.pallas_call(
        _ffn2_kernel,
        grid=(b, SEQ // TM),
        in_specs=[pl.BlockSpec((1, TM, D_MODEL), lambda i, j: (i, j, 0)),
                  pl.BlockSpec((1, 1, N_MOD, D_MODEL), lambda i, j: (i, 1, 0, 0)),
                  _const_spec((1, D_MODEL)), _const_spec((1, D_MODEL)),
                  _const_spec((D_MODEL, 2 * D_FF)), _const_spec((D_FF, D_MODEL))],
        out_specs=pl.BlockSpec((1, TM, D_MODEL), lambda i, j: (i, j, 0)),
        out_shape=jax.ShapeDtypeStruct((b, SEQ, D_MODEL), F32),
        compiler_params=pltpu.CompilerParams(
            dimension_semantics=("arbitrary", "arbitrary"), vmem_limit_bytes=VMEM_LIMIT),
        name="ffn2",
    )(x, mods, pre_g, post_g, w_in, w_out)


def _lane_tile(t, reps):
    return jnp.concatenate([t] * reps, axis=1)


def _mixin_kernel(x_ref, mod_ref, pre_ref, w_ref, cos_ref, sin_a_ref, sin_b_ref,
                  urw_ref, q_ref, k_ref, v_ref, gate_ref):
    mod = mod_ref[0, 0]
    h = (_rms(x_ref[0], pre_ref[...]) * (1.0 + mod[4:5]) + mod[3:4]).astype(BF16)
    u = jnp.dot(h, w_ref[...], preferred_element_type=F32)
    urw_ref[0] = u[:, :RWKV_COLS]
    reps = DIFF_WIDTH // 128
    cos_t = _lane_tile(cos_ref[...], reps)
    sin_a = _lane_tile(sin_a_ref[...], reps)
    sin_b = _lane_tile(sin_b_ref[...], reps)

    def rope(z):
        return (z * cos_t + pltpu.roll(z, DIFF_WIDTH - ROPE_FREQS, 1) * sin_a
                + pltpu.roll(z, ROPE_FREQS, 1) * sin_b)

    c0 = RWKV_COLS
    q_ref[0] = (rope(u[:, c0:c0 + DIFF_WIDTH]) * HEAD_DIM ** -0.5).astype(BF16)
    k_ref[0] = rope(u[:, c0 + DIFF_WIDTH:c0 + 2 * DIFF_WIDTH]).astype(BF16)
    v_ref[0] = u[:, c0 + 2 * DIFF_WIDTH:c0 + 3 * DIFF_WIDTH].astype(BF16)
    gate_ref[0] = _sigmoid(u[:, c0 + 3 * DIFF_WIDTH:]).astype(BF16)


def _mixin(x1, mods, pre_g, w, cos_t, sin_a, sin_b):
    b = x1.shape[0]
    nt = ALL_LEN // TM
    tile = lambda width: pl.BlockSpec((1, TM, width), lambda i, j: (i, j, 0))
    latent = lambda width: pl.BlockSpec((1, TM, width), lambda i, j: (i, jnp.maximum(j - 1, 0), 0))
    rope_spec = pl.BlockSpec((TM, 128), lambda i, j: (j, 0))
    return pl.pallas_call(
        _mixin_kernel,
        grid=(b, nt),
        in_specs=[tile(D_MODEL),
                  pl.BlockSpec((1, 1, N_MOD, D_MODEL), lambda i, j: (i, jnp.minimum(j, 1), 0, 0)),
                  _const_spec((1, D_MODEL)), _const_spec((D_MODEL, MIX_COLS)),
                  rope_spec, rope_spec, rope_spec],
        out_specs=[tile(RWKV_COLS), latent(DIFF_WIDTH), tile(DIFF_WIDTH), tile(DIFF_WIDTH),
                   latent(2 * D_MODEL)],
        out_shape=[jax.ShapeDtypeStruct((b, ALL_LEN, RWKV_COLS), F32),
                   jax.ShapeDtypeStruct((b, SEQ, DIFF_WIDTH), BF16),
                   jax.ShapeDtypeStruct((b, ALL_LEN, DIFF_WIDTH), BF16),
                   jax.ShapeDtypeStruct((b, ALL_LEN, DIFF_WIDTH), BF16),
                   jax.ShapeDtypeStruct((b, SEQ, 2 * D_MODEL), BF16)],
        compiler_params=pltpu.CompilerParams(
            dimension_semantics=("arbitrary", "arbitrary"), vmem_limit_bytes=VMEM_LIMIT),
        name="mix_in",
    )(x1, mods, pre_g, w, cos_t, sin_a, sin_b)


def _rope_tables():
    rows = SEQ // GRID_W
    row = jnp.repeat(jnp.arange(rows, dtype=F32), GRID_W)
    col = jnp.tile(jnp.arange(GRID_W, dtype=F32), rows)
    freqs = ROPE_BASE ** (-jnp.arange(ROPE_FREQS, dtype=F32) / ROPE_FREQS)
    p = np.arange(128) % HEAD_DIM
    f = p % ROPE_FREQS
    ang = jnp.where((p // 32)[None, :] == 0, row[:, None], col[:, None]) * freqs[f][None, :]
    first = ((p % 32) < ROPE_FREQS)[None, :]
    cos_x, sin_x = jnp.cos(ang), jnp.sin(ang)
    cos_t = jnp.concatenate([jnp.ones((CTX_LEN, 128), F32), cos_x])
    sin_a = jnp.concatenate([jnp.zeros((CTX_LEN, 128), F32), jnp.where(first, -sin_x, 0.0)])
    sin_b = jnp.concatenate([jnp.zeros((CTX_LEN, 128), F32), jnp.where(first, 0.0, sin_x)])
    return cos_t, sin_a, sin_b


_M_STRICT, _M_INCL, _M_D16, _M_O1, _M_O2, _M_EYE = range(6)


def _rwkv_masks():
    t = np.arange(CHUNK)[:, None]
    i = np.arange(CHUNK)[None, :]
    cat = lambda m: np.tile(m.astype(np.float32), (1, GROUP))
    packs, tris = [], []
    for direction in range(2):
        strict = (i < t) if direction == 0 else (i > t)
        incl = (i <= t) if direction == 0 else (i >= t)
        d16 = (t // 16 == i // 16)
        o1 = (t // 32 == i // 32) & ~d16
        o2 = (t // 32 != i // 32)
        packs.append(np.stack([cat(strict), cat(incl), cat(strict & d16), cat(strict & o1),
                               cat(strict & o2), cat(t == i)]))
        tris.append(incl.astype(np.float32))
    block = (np.arange(GW)[:, None] // HEAD_DIM == np.arange(GW)[None, :] // HEAD_DIM)
    head = (np.arange(RWKV_WIDTH)[:, None] // HEAD_DIM == np.arange(RWKV_WIDTH)[None, :] // HEAD_DIM)
    return (jnp.asarray(np.stack(packs), F32), jnp.asarray(np.stack(tris), BF16),
            jnp.asarray(block, BF16), jnp.asarray(head, BF16))


def _block_diag(x, block):
    return jnp.concatenate([x.astype(BF16)] * GROUP, axis=0) * block


def _rwkv_chunk(state_ref, grp, r, logw, kd, v, kk, a, masks, tri, block):
    bd = lambda x: _block_diag(x, block)
    cum_incl = _split_dot(logw, tri, 3, m_left=True)
    cum_excl = cum_incl - logw
    total = jnp.sum(logw, axis=0, keepdims=True)
    a_t = kk * jnp.exp(cum_excl)
    r_t = r * jnp.exp(cum_incl)
    e_inv = jnp.exp(-cum_incl)
    b_t = kk * a * e_inv
    k_t = kd * e_inv
    ar = jnp.concatenate([a_t, r_t], axis=0)
    g_b = _dot_nt(ar, bd(b_t))
    g_k = _dot_nt(ar, bd(k_t))
    l_ak = g_k[:CHUNK] * masks(_M_STRICT)
    m_rb = g_b[CHUNK:] * masks(_M_INCL)
    m_rk = g_k[CHUNK:] * masks(_M_INCL)
    n1 = -(g_b[:CHUNK] * masks(_M_D16))
    n2 = _dot(n1, bd(n1))
    n4 = _dot(n2, bd(n2))
    n8 = _dot(n4, bd(n4))
    t_inv = masks(_M_EYE) + n1
    t_inv = t_inv + _dot(t_inv, bd(n2))
    t_inv = t_inv + _dot(t_inv, bd(n4))
    t_inv = t_inv + _dot(t_inv, bd(n8))
    for m_id in (_M_O1, _M_O2):
        off = g_b[:CHUNK] * masks(m_id)
        t_inv = t_inv - _dot(_dot(t_inv, bd(off)), bd(t_inv))
    x = _dot(l_ak, bd(v))
    a_hat = _dot(t_inv, bd(a_t))
    z0 = _dot(t_inv, bd(x))
    r_hat = r_t - _dot(m_rb, bd(a_hat))
    y0 = _dot(m_rk, bd(v)) - _dot(m_rb, bd(z0))
    state = state_ref[grp]
    zy = _dot_nt(jnp.concatenate([a_hat, r_hat], axis=0), state) + jnp.concatenate([z0, y0], axis=0)
    z, y = zy[:CHUNK], zy[CHUNK:]
    vz_t = jnp.concatenate([v, -z], axis=0).T
    upd = _dot(vz_t, jnp.concatenate([k_t, b_t], axis=0))
    state_ref[grp] = (state + upd * block.astype(F32)) * jnp.exp(total)
    return y


def _rwkv_dir(direction, j, u_ref, up_ref, un_ref, state_ref, p, y_ref, bon_ref, g_ref):
    (sw, w0, w2, a0, a2, g2, k_k, k_a, r_k, masks_ref, tri_ref, block_ref, head_ref) = p
    u = u_ref[0]
    rows = lax.broadcasted_iota(jnp.int32, (CHUNK, 1), 0)
    has_prev = jnp.logical_and(j != 0, j != CTX_CHUNKS)
    has_next = jnp.logical_and(j != CTX_CHUNKS - 1, j != N_CHUNKS - 1)
    prev_row = jnp.where(has_prev, up_ref[0, HALO - 1:HALO, :], 0.0)
    next_row = jnp.where(has_next, un_ref[0, 0:1, :], 0.0)
    u_m1 = jnp.where(rows == 0, prev_row, pltpu.roll(u, 1, 0))
    u_p1 = jnp.where(rows == CHUNK - 1, next_row, pltpu.roll(u, CHUNK - 1, 0))
    uc = u_m1 * sw[0:1] + u * sw[1:2] + u_p1 * sw[2:3]

    wdt = RWKV_WIDTH
    r = uc[:, 0:wdt]
    k = uc[:, wdt:2 * wdt]
    v = uc[:, 2 * wdt:3 * wdt]
    c0 = 3 * wdt
    w_logit = w0[direction:direction + 1] + _dot(jnp.tanh(uc[:, c0:c0 + LORA_W]), w2[direction])
    logw = -math.exp(-0.5) * _sigmoid(w_logit)
    a = _sigmoid(a0[direction:direction + 1] + _dot(uc[:, c0 + LORA_W:c0 + 2 * LORA_W], a2[direction]))
    head = head_ref[...]
    kk = k * k_k[...]
    kk = kk * lax.rsqrt(_split_dot(kk * kk, head, 2) + 1e-12)
    kd = k * (1.0 + (a - 1.0) * k_a[...])
    bon_ref[0] = _split_dot(r * kd * r_k[...], head, 2) * v
    if g_ref is not None:
        g_ref[0] = _dot(_sigmoid(uc[:, c0 + 2 * LORA_W:]), g2[...])

    masks = lambda m_id: masks_ref[direction, m_id]
    tri = tri_ref[direction]
    block = block_ref[...]
    ys = []
    for grp in range(RWKV_HEADS // GROUP):
        sl = slice(grp * GW, (grp + 1) * GW)
        ys.append(_rwkv_chunk(state_ref.at[direction], grp, r[:, sl], logw[:, sl], kd[:, sl], v[:, sl],
                              kk[:, sl], a[:, sl], masks, tri, block))
    y_ref[0] = jnp.concatenate(ys, axis=1)


def _bwd_chunk(s):
    return jnp.where(s < CTX_CHUNKS, CTX_CHUNKS - 1 - s, N_CHUNKS + CTX_CHUNKS - 1 - s)


def _rwkv_kernel(uf_ref, ufp_ref, ufn_ref, ub_ref, ubp_ref, ubn_ref, *rest):
    params, outs, state_ref = rest[:13], rest[13:18], rest[18]
    y0_ref, y1_ref, bon0_ref, bon1_ref, g_ref = outs
    s = pl.program_id(1)

    @pl.when(s == 0)
    def _():
        state_ref[...] = jnp.zeros_like(state_ref)

    _rwkv_dir(0, s, uf_ref, ufp_ref, ufn_ref, state_ref, params, y0_ref, bon0_ref, g_ref)
    _rwkv_dir(1, _bwd_chunk(s), ub_ref, ubp_ref, ubn_ref, state_ref, params, y1_ref, bon1_ref, None)


def _rwkv(urw, sw, w0, w2, a0, a2, g2, k_k, k_a, r_k, consts):
    b = urw.shape[0]
    masks, tri, block, head = consts
    halo_blocks = ALL_LEN // HALO
    per_chunk = CHUNK // HALO
    fwd = lambda s: s
    def specs(chunk_of):
        return [pl.BlockSpec((1, CHUNK, RWKV_COLS), lambda i, s: (i, chunk_of(s), 0)),
                pl.BlockSpec((1, HALO, RWKV_COLS),
                             lambda i, s: (i, jnp.maximum(chunk_of(s) * per_chunk - 1, 0), 0)),
                pl.BlockSpec((1, HALO, RWKV_COLS),
                             lambda i, s: (i, jnp.minimum((chunk_of(s) + 1) * per_chunk, halo_blocks - 1), 0))]
    out_f = pl.BlockSpec((1, CHUNK, RWKV_WIDTH), lambda i, s: (i, s, 0))
    out_b = pl.BlockSpec((1, CHUNK, RWKV_WIDTH), lambda i, s: (i, _bwd_chunk(s), 0))
    shape = jax.ShapeDtypeStruct((b, ALL_LEN, RWKV_WIDTH), F32)
    param_arrays = (sw, w0, w2, a0, a2, g2, k_k, k_a, r_k, masks, tri, block, head)
    return pl.pallas_call(
        _rwkv_kernel,
        grid=(b, N_CHUNKS),
        in_specs=specs(fwd) + specs(_bwd_chunk) + [_const_spec(p.shape) for p in param_arrays],
        out_specs=[out_f, out_b, out_f, out_b, out_f],
        out_shape=[shape] * 5,
        scratch_shapes=[pltpu.VMEM((2, RWKV_HEADS // GROUP, GW, GW), F32)],
        compiler_params=pltpu.CompilerParams(
            dimension_semantics=("arbitrary", "arbitrary"), vmem_limit_bytes=VMEM_LIMIT),
        name="rwkv",
    )(urw, urw, urw, urw, urw, urw, *param_arrays)


def _attn_kernel(q_ref, k_ref, v_ref, lam_ref, g_ref, o_ref):
    q = q_ref[0]
    lane = lax.broadcasted_iota(jnp.int32, (1, DIFF_V_DIM), 1)
    first = lane < HEAD_DIM
    zero = jnp.zeros_like(q)
    q2 = jnp.concatenate([jnp.where(first, q, zero), jnp.where(first, zero, q)], axis=0)
    s = lax.dot_general(q2, k_ref[0], (((1,), (1,)), ((), ())), preferred_element_type=F32)
    e = jnp.exp(s - jnp.max(s, axis=-1, keepdims=True))
    p = e * (1.0 / jnp.sum(e, axis=-1, keepdims=True))
    lp = lam_ref[...]
    lane_sum = lambda t: jnp.sum(t, axis=-1, keepdims=True)
    lam = jnp.exp(lane_sum(lp[0:1] * lp[1:2])) - jnp.exp(lane_sum(lp[2:3] * lp[3:4])) + LAM_INIT
    attn = (p[:TQ] - lam * p[TQ:]).astype(BF16)
    o = jnp.dot(attn, v_ref[0], preferred_element_type=F32)
    o_ref[0] = (_rms(o, g_ref[...]) * (1.0 - LAM_INIT)).astype(BF16)


def _attn(q, k, v, lam_p, subln_g):
    b = q.shape[0]
    return pl.pallas_call(
        _attn_kernel,
        grid=(b, DIFF_HEADS, SEQ // TQ),
        in_specs=[pl.BlockSpec((1, TQ, DIFF_V_DIM), lambda i, h, t: (i, t, h)),
                  pl.BlockSpec((1, ALL_LEN, DIFF_V_DIM), lambda i, h, t: (i, 0, h)),
                  pl.BlockSpec((1, ALL_LEN, DIFF_V_DIM), lambda i, h, t: (i, 0, h)),
                  pl.BlockSpec((4, HEAD_DIM), lambda i, h, t: (0, 0)),
                  pl.BlockSpec((1, DIFF_V_DIM), lambda i, h, t: (0, 0))],
        out_specs=pl.BlockSpec((1, TQ, DIFF_V_DIM), lambda i, h, t: (i, t, h)),
        out_shape=jax.ShapeDtypeStruct((b, SEQ, DIFF_WIDTH), BF16),
        compiler_params=pltpu.CompilerParams(
            dimension_semantics=("arbitrary", "arbitrary", "arbitrary"), vmem_limit_bytes=VMEM_LIMIT),
        name="diff_attn",
    )(q, k, v, lam_p, subln_g)


def _merge_kernel(x_ref, mod_ref, y0_ref, y1_ref, bon0_ref, bon1_ref, g_ref, yb_ref, gate_ref,
                  avg_ref, ln_g_ref, ln_b_ref, up_a_ref, up_b_ref, w_out_ref, post_ref, o_ref):
    y = y0_ref[0] + y1_ref[0]
    avg = avg_ref[...]
    yc = y - _split_dot(y, avg, 2)
    var = _split_dot(yc * yc, avg, 2)
    yn = yc * lax.rsqrt(var + RWKV_GN_EPS) * ln_g_ref[...] + ln_b_ref[...]
    ya = ((yn + bon0_ref[0] + bon1_ref[0]) * g_ref[0]).astype(BF16)
    gates = gate_ref[0].astype(F32)
    mix = (gates[:, :D_MODEL] * jnp.dot(ya, up_a_ref[...], preferred_element_type=F32)
           + gates[:, D_MODEL:] * jnp.dot(yb_ref[0], up_b_ref[...], preferred_element_type=F32))
    out = jnp.dot(mix.astype(BF16), w_out_ref[...], preferred_element_type=F32)
    o_ref[0] = x_ref[0] + mod_ref[0, 0, 5:6] * _rms(out, post_ref[...])


def _merge(x1, mods, y0, y1, bon0, bon1, g, yb, gates, avg, ln_g, ln_b, up_a, up_b, w_out, post_g):
    b = x1.shape[0]
    off = CTX_LEN // TM
    shifted = lambda width: pl.BlockSpec((1, TM, width), lambda i, j: (i, j + off, 0))
    return pl.pallas_call(
        _merge_kernel,
        grid=(b, SEQ // TM),
        in_specs=[shifted(D_MODEL),
                  pl.BlockSpec((1, 1, N_MOD, D_MODEL), lambda i, j: (i, 1, 0, 0)),
                  shifted(RWKV_WIDTH), shifted(RWKV_WIDTH), shifted(RWKV_WIDTH), shifted(RWKV_WIDTH),
                  shifted(RWKV_WIDTH),
                  pl.BlockSpec((1, TM, DIFF_WIDTH), lambda i, j: (i, j, 0)),
                  pl.BlockSpec((1, TM, 2 * D_MODEL), lambda i, j: (i, j, 0)),
                  _const_spec((RWKV_WIDTH, RWKV_WIDTH)), _const_spec((1, RWKV_WIDTH)),
                  _const_spec((1, RWKV_WIDTH)), _const_spec((RWKV_WIDTH, D_MODEL)),
                  _const_spec((DIFF_WIDTH, D_MODEL)), _const_spec((D_MODEL, D_MODEL)),
                  _const_spec((1, D_MODEL))],
        out_specs=pl.BlockSpec((1, TM, D_MODEL), lambda i, j: (i, j, 0)),
        out_shape=jax.ShapeDtypeStruct((b, SEQ, D_MODEL), F32),
        compiler_params=pltpu.CompilerParams(
            dimension_semantics=("arbitrary", "arbitrary"), vmem_limit_bytes=VMEM_LIMIT),
        name="merge",
    )(x1, mods, y0, y1, bon0, bon1, g, yb, gates, avg, ln_g, ln_b, up_a, up_b, w_out, post_g)


def kernel(x, c, ctx, c_ctx, ada_w, ada_b, pre_norm_g, post_norm_g, ffn1_w_in, ffn1_w_out, mix_w_in,
           rwkv_shift_w, rwkv_w0, rwkv_w2, rwkv_a0, rwkv_a2, rwkv_g2, rwkv_k_k, rwkv_k_a, rwkv_r_k,
           rwkv_ln_g, rwkv_ln_b, diff_lambda, diff_subln_g, branch_up_a, branch_up_b, mix_w_out,
           ffn2_w_in, ffn2_w_out):
    assert x.shape[1:] == (SEQ, D_MODEL) and ctx.shape[1:] == (CTX_LEN, D_MODEL)
    assert ada_w.shape[0] == 1, "single-layer stack"
    b = x.shape[0]
    row = lambda t: t.reshape(1, -1)

    pad = (-(b + 1)) % 8
    cond = jnp.concatenate([c, c_ctx[None], jnp.zeros((pad, D_MODEL), F32)], axis=0)
    m = _adaln(cond, ada_w[0], ada_b[0]).reshape(b + 1 + pad, N_MOD, D_MODEL)
    mods = jnp.stack([jnp.broadcast_to(m[b], (b, N_MOD, D_MODEL)), m[:b]], axis=1)

    x1 = _ffn1(ctx, x, mods, row(pre_norm_g[0, 0]), row(post_norm_g[0, 0]),
               ffn1_w_in[0].astype(BF16), ffn1_w_out[0].astype(BF16))

    urw, q, k, v, gates = _mixin(x1, mods, row(pre_norm_g[0, 1]), mix_w_in[0].astype(BF16), *_rope_tables())

    zeros = jnp.zeros((LORA_W // 2, RWKV_WIDTH), F32)
    pad_dir = lambda w: jnp.stack([jnp.concatenate([w[0], zeros]), jnp.concatenate([zeros, w[1]])]).astype(BF16)
    consts = _rwkv_masks()
    y0, y1, bon0, bon1, g = _rwkv(
        urw, rwkv_shift_w[0], rwkv_w0[0], pad_dir(rwkv_w2[0]), rwkv_a0[0], pad_dir(rwkv_a2[0]),
        rwkv_g2[0].astype(BF16), row(rwkv_k_k[0]), row(rwkv_k_a[0]), row(rwkv_r_k[0]), consts)

    yb = _attn(q, k, v, diff_lambda[0], row(diff_subln_g[0]))

    avg = (consts[3].astype(F32) / HEAD_DIM).astype(BF16)
    x2 = _merge(x1, mods, y0, y1, bon0, bon1, g, yb, gates, avg, row(rwkv_ln_g[0]), row(rwkv_ln_b[0]),
                branch_up_a[0].astype(BF16), branch_up_b[0].astype(BF16), mix_w_out[0].astype(BF16),
                row(post_norm_g[0, 1]))

    return _ffn2(x2, mods, row(pre_norm_g[0, 2]), row(post_norm_g[0, 2]),
                 ffn2_w_in[0].astype(BF16), ffn2_w_out[0].astype(BF16))
```

```python
import functools
import math

import numpy as np
import jax
import jax.numpy as jnp
from jax import lax
from jax.experimental import pallas as pl
from jax.experimental.pallas import tpu as pltpu

D_MODEL = 1024
SEQ = 2048
CTX_LEN = 256
ALL_LEN = CTX_LEN + SEQ
GRID_W = 64
N_MOD = 9
D_FF = 2816
RMS_EPS = 1e-6

RWKV_HEADS = 8
HEAD_DIM = 64
RWKV_WIDTH = 512
LORA_W = 128
GATE_LORA = 128
RWKV_COLS = 1920
RWKV_GN_EPS = 64e-5

DIFF_HEADS = 4
DIFF_WIDTH = 512
DIFF_V_DIM = 128
ROPE_BASE = 10000.0
ROPE_FREQS = 16
LAM_INIT = 0.8 - 0.6 * math.exp(-0.0)

MIX_COLS = RWKV_COLS + 3 * DIFF_WIDTH + 2 * D_MODEL

TM = 256
CHUNK = 64
GROUP = 4
GW = GROUP * HEAD_DIM
N_CHUNKS = ALL_LEN // CHUNK
CTX_CHUNKS = CTX_LEN // CHUNK
HALO = 8
TQ = 512

F32 = jnp.float32
BF16 = jnp.bfloat16
VMEM_LIMIT = 56 * 1024 * 1024


def _const_spec(shape):
    return pl.BlockSpec(shape, lambda *_: (0,) * len(shape), pipeline_mode=pl.Buffered(1))


def _sigmoid(x):
    return 1.0 / (1.0 + jnp.exp(-x))


def _rms(x, g):
    return x * lax.rsqrt(jnp.mean(x * x, axis=-1, keepdims=True) + RMS_EPS) * g


def _dot(a, b):
    return jnp.dot(a.astype(BF16), b.astype(BF16), preferred_element_type=F32)


def _dot_nt(a, b):
    return lax.dot_general(a.astype(BF16), b.astype(BF16), (((1,), (1,)), ((), ())),
                           preferred_element_type=F32)


def _split_dot(x, m, terms, m_left=False):
    acc = None
    rem = x
    for _ in range(terms):
        part = rem.astype(BF16)
        d = jnp.dot(*((m, part) if m_left else (part, m)), preferred_element_type=F32)
        acc = d if acc is None else acc + d
        rem = rem - part.astype(F32)
    return acc


def _ada_kernel(cond_ref, w_ref, b_ref, o_ref):
    c = cond_ref[...]
    o_ref[...] = _dot(c * _sigmoid(c), w_ref[...]) + b_ref[...]


def _adaln(cond, w, b):
    rows = cond.shape[0]
    n = w.shape[1]
    tn = 1152
    return pl.pallas_call(
        _ada_kernel,
        grid=(n // tn,),
        in_specs=[pl.BlockSpec((rows, D_MODEL), lambda i: (0, 0)),
                  pl.BlockSpec((D_MODEL, tn), lambda i: (0, i)),
                  pl.BlockSpec((1, tn), lambda i: (0, i))],
        out_specs=pl.BlockSpec((rows, tn), lambda i: (0, i)),
        out_shape=jax.ShapeDtypeStruct((rows, n), F32),
        name="adaln",
    )(cond, w, b.reshape(1, n))


def _ffn_core(x, mod, pre_g, post_g, w_in_ref, w_out_ref):
    h = (_rms(x, pre_g) * (1.0 + mod[1:2]) + mod[0:1]).astype(BF16)
    gu = jnp.dot(h, w_in_ref[...], preferred_element_type=F32)
    gate, up = gu[:, :D_FF], gu[:, D_FF:]
    act = (gate * _sigmoid(gate) * up).astype(BF16)
    o = jnp.dot(act, w_out_ref[...], preferred_element_type=F32)
    return x + 0.5 * mod[2:3] * _rms(o, post_g)


def _ffn1_kernel(ctx_ref, x_ref, mod_ref, pre_ref, post_ref, w_in_ref, w_out_ref, o_ref):
    is_ctx = pl.program_id(1) == 0
    x = jnp.where(is_ctx, ctx_ref[0], x_ref[0])
    o_ref[0] = _ffn_core(x, mod_ref[0, 0, 0:3], pre_ref[...], post_ref[...], w_in_ref, w_out_ref)


def _ffn1(ctx, x, mods, pre_g, post_g, w_in, w_out):
    b = x.shape[0]
    nt = ALL_LEN // TM
    return pl.pallas_call(
        _ffn1_kernel,
        grid=(b, nt),
        in_specs=[pl.BlockSpec((1, TM, D_MODEL), lambda i, j: (i, 0, 0)),
                  pl.BlockSpec((1, TM, D_MODEL), lambda i, j: (i, jnp.maximum(j - 1, 0), 0)),
                  pl.BlockSpec((1, 1, N_MOD, D_MODEL), lambda i, j: (i, jnp.minimum(j, 1), 0, 0)),
                  _const_spec((1, D_MODEL)), _const_spec((1, D_MODEL)),
                  _const_spec((D_MODEL, 2 * D_FF)), _const_spec((D_FF, D_MODEL))],
        out_specs=pl.BlockSpec((1, TM, D_MODEL), lambda i, j: (i, j, 0)),
        out_shape=jax.ShapeDtypeStruct((b, ALL_LEN, D_MODEL), F32),
        compiler_params=pltpu.CompilerParams(
            dimension_semantics=("arbitrary", "arbitrary"), vmem_limit_bytes=VMEM_LIMIT),
        name="ffn1",
    )(ctx, x, mods, pre_g, post_g, w_in, w_out)


def _ffn2_kernel(x_ref, mod_ref, pre_ref, post_ref, w_in_ref, w_out_ref, o_ref):
    o_ref[0] = _ffn_core(x_ref[0], mod_ref[0, 0, 6:9], pre_ref[...], post_ref[...], w_in_ref, w_out_ref)


def _ffn2(x, mods, pre_g, post_g, w_in, w_out):
    b = x.shape[0]
    return pl.pallas_call(
        _ffn2_kernel,
        grid=(b, SEQ // TM),
        in_specs=[pl.BlockSpec((1, TM, D_MODEL), lambda i, j: (i, j, 0)),
                  pl.BlockSpec((1, 1, N_MOD, D_MODEL), lambda i, j: (i, 1, 0, 0)),
                  _const_spec((1, D_MODEL)), _const_spec((1, D_MODEL)),
                  _const_spec((D_MODEL, 2 * D_FF)), _const_spec((D_FF, D_MODEL))],
        out_specs=pl.BlockSpec((1, TM, D_MODEL), lambda i, j: (i, j, 0)),
        out_shape=jax.ShapeDtypeStruct((b, SEQ, D_MODEL), F32),
        compiler_params=pltpu.CompilerParams(
            dimension_semantics=("arbitrary", "arbitrary"), vmem_limit_bytes=VMEM_LIMIT),
        name="ffn2",
    )(x, mods, pre_g, post_g, w_in, w_out)


def _lane_tile(t, reps):
    return jnp.concatenate([t] * reps, axis=1)


def _mixin_kernel(x_ref, mod_ref, pre_ref, w_ref, cos_ref, sin_a_ref, sin_b_ref,
                  urw_ref, q_ref, k_ref, vt_ref, gate_ref):
    mod = mod_ref[0, 0]
    h = (_rms(x_ref[0], pre_ref[...]) * (1.0 + mod[4:5]) + mod[3:4]).astype(BF16)
    u = jnp.dot(h, w_ref[...], preferred_element_type=F32)
    urw_ref[0] = u[:, :RWKV_COLS]
    reps = DIFF_WIDTH // 128
    cos_t = _lane_tile(cos_ref[...], reps)
    sin_a = _lane_tile(sin_a_ref[...], reps)
    sin_b = _lane_tile(sin_b_ref[...], reps)

    def rope(z):
        return (z * cos_t + pltpu.roll(z, DIFF_WIDTH - ROPE_FREQS, 1) * sin_a
                + pltpu.roll(z, ROPE_FREQS, 1) * sin_b)

    c0 = RWKV_COLS
    q_ref[0] = (rope(u[:, c0:c0 + DIFF_WIDTH]) * HEAD_DIM ** -0.5).astype(BF16)
    k_ref[0] = rope(u[:, c0 + DIFF_WIDTH:c0 + 2 * DIFF_WIDTH]).astype(BF16)
    vt_ref[0] = u[:, c0 + 2 * DIFF_WIDTH:c0 + 3 * DIFF_WIDTH].T.astype(BF16)
    gate_ref[0] = _sigmoid(u[:, c0 + 3 * DIFF_WIDTH:]).astype(BF16)


def _mixin(x1, mods, pre_g, w, cos_t, sin_a, sin_b):
    b = x1.shape[0]
    nt = ALL_LEN // TM
    tile = lambda width: pl.BlockSpec((1, TM, width), lambda i, j: (i, j, 0))
    latent = lambda width: pl.BlockSpec((1, TM, width), lambda i, j: (i, jnp.maximum(j - 1, 0), 0))
    rope_spec = pl.BlockSpec((TM, 128), lambda i, j: (j, 0))
    return pl.pallas_call(
        _mixin_kernel,
        grid=(b, nt),
        in_specs=[tile(D_MODEL),
                  pl.BlockSpec((1, 1, N_MOD, D_MODEL), lambda i, j: (i, jnp.minimum(j, 1), 0, 0)),
                  _const_spec((1, D_MODEL)), _const_spec((D_MODEL, MIX_COLS)),
                  rope_spec, rope_spec, rope_spec],
        out_specs=[tile(RWKV_COLS), latent(DIFF_WIDTH), tile(DIFF_WIDTH),
                   pl.BlockSpec((1, DIFF_WIDTH, TM), lambda i, j: (i, 0, j)),
                   latent(2 * D_MODEL)],
        out_shape=[jax.ShapeDtypeStruct((b, ALL_LEN, RWKV_COLS), F32),
                   jax.ShapeDtypeStruct((b, SEQ, DIFF_WIDTH), BF16),
                   jax.ShapeDtypeStruct((b, ALL_LEN, DIFF_WIDTH), BF16),
                   jax.ShapeDtypeStruct((b, DIFF_WIDTH, ALL_LEN), BF16),
                   jax.ShapeDtypeStruct((b, SEQ, 2 * D_MODEL), BF16)],
        compiler_params=pltpu.CompilerParams(
            dimension_semantics=("arbitrary", "arbitrary"), vmem_limit_bytes=VMEM_LIMIT),
        name="mix_in",
    )(x1, mods, pre_g, w, cos_t, sin_a, sin_b)


def _rope_tables():
    rows = SEQ // GRID_W
    row = jnp.repeat(jnp.arange(rows, dtype=F32), GRID_W)
    col = jnp.tile(jnp.arange(GRID_W, dtype=F32), rows)
    freqs = ROPE_BASE ** (-jnp.arange(ROPE_FREQS, dtype=F32) / ROPE_FREQS)
    p = np.arange(128) % HEAD_DIM
    f = p % ROPE_FREQS
    ang = jnp.where((p // 32)[None, :] == 0, row[:, None], col[:, None]) * freqs[f][None, :]
    first = ((p % 32) < ROPE_FREQS)[None, :]
    cos_x, sin_x = jnp.cos(ang), jnp.sin(ang)
    cos_t = jnp.concatenate([jnp.ones((CTX_LEN, 128), F32), cos_x])
    sin_a = jnp.concatenate([jnp.zeros((CTX_LEN, 128), F32), jnp.where(first, -sin_x, 0.0)])
    sin_b = jnp.concatenate([jnp.zeros((CTX_LEN, 128), F32), jnp.where(first, 0.0, sin_x)])
    return cos_t, sin_a, sin_b


_M_STRICT, _M_INCL, _M_D16, _M_OFF, _M_EYE = range(5)
BPS = 2
N_GROUPS = RWKV_HEADS // GROUP
N_INST = 2 * BPS * N_GROUPS


def _rwkv_masks():
    t = np.arange(CHUNK)[:, None]
    i = np.arange(CHUNK)[None, :]
    cat = lambda m: np.tile(m.astype(np.float32), (1, GROUP))
    packs, tris = [], []
    for direction in range(2):
        strict = (i < t) if direction == 0 else (i > t)
        incl = (i <= t) if direction == 0 else (i >= t)
        d16 = (t // 16 == i // 16)
        packs.append(np.stack([cat(strict), cat(incl), cat(strict & d16), cat(strict & ~d16), cat(t == i)]))
        tris.append(np.kron(np.eye(BPS), incl).astype(np.float32))
    block = (np.arange(GW)[:, None] // HEAD_DIM == np.arange(GW)[None, :] // HEAD_DIM)
    head = (np.arange(RWKV_WIDTH)[:, None] // HEAD_DIM == np.arange(RWKV_WIDTH)[None, :] // HEAD_DIM)
    return (jnp.asarray(np.stack(packs), F32), jnp.asarray(np.stack(tris), BF16),
            jnp.asarray(block, BF16), jnp.asarray(head, BF16))


def _bmm(a, b):
    return jnp.einsum("imk,ikn->imn", a.astype(BF16), b.astype(BF16), preferred_element_type=F32)


def _bmm_nt(a, b):
    return jnp.einsum("imk,ink->imn", a.astype(BF16), b.astype(BF16), preferred_element_type=F32)


def _rwkv_chunks(state_ref, fwd, bwd, masks_ref, block):
    c = CHUNK
    inst = lambda name: jnp.stack([d[name][e * c:(e + 1) * c, g * GW:(g + 1) * GW]
                                   for d in (fwd, bwd) for e in range(BPS) for g in range(N_GROUPS)])
    per_dir = N_INST // 2
    mask = lambda m_id: jnp.stack([masks_ref[0, m_id]] * per_dir + [masks_ref[1, m_id]] * per_dir)
    bd = lambda x: jnp.concatenate([x.astype(BF16)] * GROUP, axis=1) * block
    rows = lambda *parts: jnp.concatenate(parts, axis=1)
    cols = lambda *parts: jnp.concatenate(parts, axis=2)
    r, logw, cum_incl, kd, v, kk, a = map(inst, ("r", "logw", "cum", "kd", "v", "kk", "a"))
    a_t = kk * jnp.exp(cum_incl - logw)
    r_t = r * jnp.exp(cum_incl)
    e_inv = jnp.exp(-cum_incl)
    b_t = kk * a * e_inv
    k_t = kd * e_inv
    g = _bmm_nt(rows(a_t, r_t), rows(bd(b_t), bd(k_t)))
    g_ab = g[:, :c, :GW]
    l_ak = g[:, :c, GW:] * mask(_M_STRICT)
    m_rb = g[:, c:, :GW] * mask(_M_INCL)
    m_rk = g[:, c:, GW:] * mask(_M_INCL)
    n1 = -(g_ab * mask(_M_D16))
    n2 = _bmm(n1, bd(n1))
    t_d = mask(_M_EYE) + n1
    p = _bmm(rows(t_d, n2), bd(n2))
    t_d, n4 = t_d + p[:, :c], p[:, c:]
    p = _bmm(rows(t_d, n4), bd(n4))
    t_d, n8 = t_d + p[:, :c], p[:, c:]
    t_d = t_d + _bmm(t_d, bd(n8))
    m1 = -_bmm(g_ab * mask(_M_OFF), bd(t_d))
    p = _bmm(rows(t_d, m1), bd(m1))
    q, m2 = t_d + p[:, :c], p[:, c:]
    t_inv = q + _bmm(q, bd(m2))
    p = _bmm(rows(l_ak, m_rk), bd(v))
    x, y_v = p[:, :c], p[:, c:]
    az = _bmm(t_inv, cols(bd(a_t), bd(x)))
    a_hat, z0 = az[:, :, :GW], az[:, :, GW:]
    rz = _bmm(m_rb, cols(bd(a_hat), bd(z0)))
    r_hat, y0 = r_t - rz[:, :, :GW], y_v - rz[:, :, GW:]
    state = state_ref[...]
    zy = _bmm_nt(rows(a_hat, r_hat), state) + rows(z0, y0)
    z, y = zy[:, :c], zy[:, c:]
    vz = rows(v, -z)
    kb = rows(k_t, b_t)
    upd = jnp.stack([_dot(vz[i].T, kb[i]) for i in range(N_INST)])
    total = jnp.sum(logw, axis=1, keepdims=True)
    state_ref[...] = (state + upd * block.astype(F32)) * jnp.exp(total)
    return y


def _rwkv_streams(direction, j, u_ref, up_ref, un_ref, p, bon_ref, g_ref):
    (sw, w0, w2, a0, a2, g2, k_k, k_a, r_k, masks_ref, tri_ref, block_ref, head_ref) = p
    rows = lax.broadcasted_iota(jnp.int32, (CHUNK, 1), 0)
    has_prev = jnp.logical_and(j != 0, j != CTX_CHUNKS)
    has_next = jnp.logical_and(j != CTX_CHUNKS - 1, j != N_CHUNKS - 1)
    convs = []
    for e in range(BPS):
        u = u_ref[e]
        prev_row = jnp.where(has_prev, up_ref[e, HALO - 1:HALO, :], 0.0)
        next_row = jnp.where(has_next, un_ref[e, 0:1, :], 0.0)
        u_m1 = jnp.where(rows == 0, prev_row, pltpu.roll(u, 1, 0))
        u_p1 = jnp.where(rows == CHUNK - 1, next_row, pltpu.roll(u, CHUNK - 1, 0))
        convs.append(u_m1 * sw[0:1] + u * sw[1:2] + u_p1 * sw[2:3])
    uc = jnp.concatenate(convs, axis=0)

    wdt = RWKV_WIDTH
    r = uc[:, 0:wdt]
    k = uc[:, wdt:2 * wdt]
    v = uc[:, 2 * wdt:3 * wdt]
    c0 = 3 * wdt
    w_logit = w0[direction:direction + 1] + _dot(jnp.tanh(uc[:, c0:c0 + LORA_W]), w2[direction])
    logw = -math.exp(-0.5) * _sigmoid(w_logit)
    a = _sigmoid(a0[direction:direction + 1] + _dot(uc[:, c0 + LORA_W:c0 + 2 * LORA_W], a2[direction]))
    head = head_ref[...]
    kk = k * k_k[...]
    kk = kk * lax.rsqrt(_split_dot(kk * kk, head, 2) + 1e-12)
    kd = k * (1.0 + (a - 1.0) * k_a[...])
    bon = _split_dot(r * kd * r_k[...], head, 2) * v
    gate = _dot(_sigmoid(uc[:, c0 + 2 * LORA_W:]), g2[...]) if g_ref is not None else None
    for e in range(BPS):
        bon_ref[e] = bon[e * CHUNK:(e + 1) * CHUNK]
        if g_ref is not None:
            g_ref[e] = gate[e * CHUNK:(e + 1) * CHUNK]

    cum = _split_dot(logw, tri_ref[direction], 3, m_left=True)
    return dict(r=r, logw=logw, cum=cum, kd=kd, v=v, kk=kk, a=a)


def _bwd_chunk(s):
    return jnp.where(s < CTX_CHUNKS, CTX_CHUNKS - 1 - s, N_CHUNKS + CTX_CHUNKS - 1 - s)


def _rwkv_kernel(uf_ref, ufp_ref, ufn_ref, ub_ref, ubp_ref, ubn_ref, *rest):
    params, outs, state_ref = rest[:13], rest[13:18], rest[18]
    y0_ref, y1_ref, bon0_ref, bon1_ref, g_ref = outs
    s = pl.program_id(1)

    @pl.when(s == 0)
    def _():
        state_ref[...] = jnp.zeros_like(state_ref)

    fwd = _rwkv_streams(0, s, uf_ref, ufp_ref, ufn_ref, params, bon0_ref, g_ref)
    bwd = _rwkv_streams(1, _bwd_chunk(s), ub_ref, ubp_ref, ubn_ref, params, bon1_ref, None)
    y = _rwkv_chunks(state_ref, fwd, bwd, params[9], params[11][...])
    for d, y_ref in enumerate((y0_ref, y1_ref)):
        for e in range(BPS):
            first = (d * BPS + e) * N_GROUPS
            y_ref[e] = jnp.concatenate([y[first + g] for g in range(N_GROUPS)], axis=1)


def _rwkv(urw, sw, w0, w2, a0, a2, g2, k_k, k_a, r_k, consts):
    b = urw.shape[0]
    masks, tri, block, head = consts
    halo_blocks = ALL_LEN // HALO
    per_chunk = CHUNK // HALO
    fwd = lambda s: s
    def specs(chunk_of):
        return [pl.BlockSpec((BPS, CHUNK, RWKV_COLS), lambda i, s: (i, chunk_of(s), 0)),
                pl.BlockSpec((BPS, HALO, RWKV_COLS),
                             lambda i, s: (i, jnp.maximum(chunk_of(s) * per_chunk - 1, 0), 0)),
                pl.BlockSpec((BPS, HALO, RWKV_COLS),
                             lambda i, s: (i, jnp.minimum((chunk_of(s) + 1) * per_chunk, halo_blocks - 1), 0))]
    out_f = pl.BlockSpec((BPS, CHUNK, RWKV_WIDTH), lambda i, s: (i, s, 0))
    out_b = pl.BlockSpec((BPS, CHUNK, RWKV_WIDTH), lambda i, s: (i, _bwd_chunk(s), 0))
    shape = jax.ShapeDtypeStruct((b, ALL_LEN, RWKV_WIDTH), F32)
    param_arrays = (sw, w0, w2, a0, a2, g2, k_k, k_a, r_k, masks, tri, block, head)
    return pl.pallas_call(
        _rwkv_kernel,
        grid=(b // BPS, N_CHUNKS),
        in_specs=specs(fwd) + specs(_bwd_chunk) + [_const_spec(p.shape) for p in param_arrays],
        out_specs=[out_f, out_b, out_f, out_b, out_f],
        out_shape=[shape] * 5,
        scratch_shapes=[pltpu.VMEM((N_INST, GW, GW), F32)],
        compiler_params=pltpu.CompilerParams(
            dimension_semantics=("arbitrary", "arbitrary"), vmem_limit_bytes=VMEM_LIMIT),
        name="rwkv",
    )(urw, urw, urw, urw, urw, urw, *param_arrays)


def _attn_kernel(q_ref, k_ref, vt_ref, lam_ref, g_ref, o_ref):
    q = q_ref[0]
    lane = lax.broadcasted_iota(jnp.int32, (1, DIFF_V_DIM), 1)
    first = lane < HEAD_DIM
    zero = jnp.zeros_like(q)
    q2 = jnp.concatenate([jnp.where(first, q, zero), jnp.where(first, zero, q)], axis=0)
    st = lax.dot_general(k_ref[0], q2, (((1,), (1,)), ((), ())), preferred_element_type=F32)
    e = jnp.exp(st - jnp.max(st, axis=0, keepdims=True))
    inv_l = 1.0 / jnp.sum(e, axis=0, keepdims=True)
    ot = jnp.dot(vt_ref[0], e.astype(BF16), preferred_element_type=F32) * inv_l
    lp = lam_ref[...]
    lane_sum = lambda t: jnp.sum(t, axis=-1, keepdims=True)
    lam = jnp.exp(lane_sum(lp[0:1] * lp[1:2])) - jnp.exp(lane_sum(lp[2:3] * lp[3:4])) + LAM_INIT
    o = ot[:, :TQ] - lam * ot[:, TQ:]
    o = o * lax.rsqrt(jnp.mean(o * o, axis=0, keepdims=True) + RMS_EPS) * g_ref[...] * (1.0 - LAM_INIT)
    o_ref[0] = o.T.astype(BF16)


def _attn(q, k, vt, lam_p, subln_g):
    b = q.shape[0]
    return pl.pallas_call(
        _attn_kernel,
        grid=(b, DIFF_HEADS, SEQ // TQ),
        in_specs=[pl.BlockSpec((1, TQ, DIFF_V_DIM), lambda i, h, t: (i, t, h)),
                  pl.BlockSpec((1, ALL_LEN, DIFF_V_DIM), lambda i, h, t: (i, 0, h)),
                  pl.BlockSpec((1, DIFF_V_DIM, ALL_LEN), lambda i, h, t: (i, h, 0)),
                  pl.BlockSpec((4, HEAD_DIM), lambda i, h, t: (0, 0)),
                  pl.BlockSpec((DIFF_V_DIM, 1), lambda i, h, t: (0, 0))],
        out_specs=pl.BlockSpec((1, TQ, DIFF_V_DIM), lambda i, h, t: (i, t, h)),
        out_shape=jax.ShapeDtypeStruct((b, SEQ, DIFF_WIDTH), BF16),
        compiler_params=pltpu.CompilerParams(
            dimension_semantics=("arbitrary", "arbitrary", "arbitrary"), vmem_limit_bytes=VMEM_LIMIT),
        name="diff_attn",
    )(q, k, vt, lam_p, subln_g)


def _merge_kernel(x_ref, mod_ref, y0_ref, y1_ref, bon0_ref, bon1_ref, g_ref, yb_ref, gate_ref,
                  avg_ref, ln_g_ref, ln_b_ref, up_a_ref, up_b_ref, w_out_ref, post_ref, o_ref):
    y = y0_ref[0] + y1_ref[0]
    avg = avg_ref[...]
    yc = y - _split_dot(y, avg, 2)
    var = _split_dot(yc * yc, avg, 2)
    yn = yc * lax.rsqrt(var + RWKV_GN_EPS) * ln_g_ref[...] + ln_b_ref[...]
    ya = ((yn + bon0_ref[0] + bon1_ref[0]) * g_ref[0]).astype(BF16)
    gates = gate_ref[0].astype(F32)
    mix = (gates[:, :D_MODEL] * jnp.dot(ya, up_a_ref[...], preferred_element_type=F32)
           + gates[:, D_MODEL:] * jnp.dot(yb_ref[0], up_b_ref[...], preferred_element_type=F32))
    out = jnp.dot(mix.astype(BF16), w_out_ref[...], preferred_element_type=F32)
    o_ref[0] = x_ref[0] + mod_ref[0, 0, 5:6] * _rms(out, post_ref[...])


def _merge(x1, mods, y0, y1, bon0, bon1, g, yb, gates, avg, ln_g, ln_b, up_a, up_b, w_out, post_g):
    b = x1.shape[0]
    off = CTX_LEN // TM
    shifted = lambda width: pl.BlockSpec((1, TM, width), lambda i, j: (i, j + off, 0))
    return pl.pallas_call(
        _merge_kernel,
        grid=(b, SEQ // TM),
        in_specs=[shifted(D_MODEL),
                  pl.BlockSpec((1, 1, N_MOD, D_MODEL), lambda i, j: (i, 1, 0, 0)),
                  shifted(RWKV_WIDTH), shifted(RWKV_WIDTH), shifted(RWKV_WIDTH), shifted(RWKV_WIDTH),
                  shifted(RWKV_WIDTH),
                  pl.BlockSpec((1, TM, DIFF_WIDTH), lambda i, j: (i, j, 0)),
                  pl.BlockSpec((1, TM, 2 * D_MODEL), lambda i, j: (i, j, 0)),
                  _const_spec((RWKV_WIDTH, RWKV_WIDTH)), _const_spec((1, RWKV_WIDTH)),
                  _const_spec((1, RWKV_WIDTH)), _const_spec((RWKV_WIDTH, D_MODEL)),
                  _const_spec((DIFF_WIDTH, D_MODEL)), _const_spec((D_MODEL, D_MODEL)),
                  _const_spec((1, D_MODEL))],
        out_specs=pl.BlockSpec((1, TM, D_MODEL), lambda i, j: (i, j, 0)),
        out_shape=jax.ShapeDtypeStruct((b, SEQ, D_MODEL), F32),
        compiler_params=pltpu.CompilerParams(
            dimension_semantics=("arbitrary", "arbitrary"), vmem_limit_bytes=VMEM_LIMIT),
        name="merge",
    )(x1, mods, y0, y1, bon0, bon1, g, yb, gates, avg, ln_g, ln_b, up_a, up_b, w_out, post_g)


def kernel(x, c, ctx, c_ctx, ada_w, ada_b, pre_norm_g, post_norm_g, ffn1_w_in, ffn1_w_out, mix_w_in,
           rwkv_shift_w, rwkv_w0, rwkv_w2, rwkv_a0, rwkv_a2, rwkv_g2, rwkv_k_k, rwkv_k_a, rwkv_r_k,
           rwkv_ln_g, rwkv_ln_b, diff_lambda, diff_subln_g, branch_up_a, branch_up_b, mix_w_out,
           ffn2_w_in, ffn2_w_out):
    assert x.shape[1:] == (SEQ, D_MODEL) and ctx.shape[1:] == (CTX_LEN, D_MODEL)
    assert ada_w.shape[0] == 1, "single-layer stack"
    b = x.shape[0]
    assert b % BPS == 0
    row = lambda t: t.reshape(1, -1)

    pad = (-(b + 1)) % 8
    cond = jnp.concatenate([c, c_ctx[None], jnp.zeros((pad, D_MODEL), F32)], axis=0)
    m = _adaln(cond, ada_w[0], ada_b[0]).reshape(b + 1 + pad, N_MOD, D_MODEL)
    mods = jnp.stack([jnp.broadcast_to(m[b], (b, N_MOD, D_MODEL)), m[:b]], axis=1)

    x1 = _ffn1(ctx, x, mods, row(pre_norm_g[0, 0]), row(post_norm_g[0, 0]),
               ffn1_w_in[0].astype(BF16), ffn1_w_out[0].astype(BF16))

    urw, q, k, vt, gates = _mixin(x1, mods, row(pre_norm_g[0, 1]), mix_w_in[0].astype(BF16), *_rope_tables())

    zeros = jnp.zeros((LORA_W // 2, RWKV_WIDTH), F32)
    pad_dir = lambda w: jnp.stack([jnp.concatenate([w[0], zeros]), jnp.concatenate([zeros, w[1]])]).astype(BF16)
    consts = _rwkv_masks()
    y0, y1, bon0, bon1, g = _rwkv(
        urw, rwkv_shift_w[0], rwkv_w0[0], pad_dir(rwkv_w2[0]), rwkv_a0[0], pad_dir(rwkv_a2[0]),
        rwkv_g2[0].astype(BF16), row(rwkv_k_k[0]), row(rwkv_k_a[0]), row(rwkv_r_k[0]), consts)

    yb = _attn(q, k, vt, diff_lambda[0], diff_subln_g[0].reshape(DIFF_V_DIM, 1))

    avg = (consts[3].astype(F32) / HEAD_DIM).astype(BF16)
    x2 = _merge(x1, mods, y0, y1, bon0, bon1, g, yb, gates, avg, row(rwkv_ln_g[0]), row(rwkv_ln_b[0]),
                branch_up_a[0].astype(BF16), branch_up_b[0].astype(BF16), mix_w_out[0].astype(BF16),
                row(post_norm_g[0, 1]))

    return _ffn2(x2, mods, row(pre_norm_g[0, 2]), row(post_norm_g[0, 2]),
                 ffn2_w_in[0].astype(BF16), ffn2_w_out[0].astype(BF16))
```

```python
import math

import numpy as np
import jax
import jax.numpy as jnp
from jax import lax
from jax.experimental import pallas as pl
from jax.experimental.pallas import tpu as pltpu

D_MODEL = 1024
SEQ = 2048
CTX_LEN = 256
ALL_LEN = CTX_LEN + SEQ
GRID_W = 64
N_MOD = 9
D_FF = 2816
RMS_EPS = 1e-6

RWKV_HEADS = 8
HEAD_DIM = 64
RWKV_WIDTH = 512
LORA_W = 128
RWKV_COLS = 1920
RWKV_GN_EPS = 64e-5

DIFF_HEADS = 4
DIFF_WIDTH = 512
DIFF_V_DIM = 128
ROPE_BASE = 10000.0
ROPE_FREQS = 16
LAM_INIT = 0.8 - 0.6 * math.exp(-0.0)

MIX_COLS = RWKV_COLS + 3 * DIFF_WIDTH + 2 * D_MODEL

TM = 256
CHUNK = 64
GROUP = 4
GW = GROUP * HEAD_DIM
N_CHUNKS = ALL_LEN // CHUNK
CTX_CHUNKS = CTX_LEN // CHUNK
HALO = 8
TQ = 1024
Q_COLS = 512
LOG2_E = math.log2(math.e)

F32 = jnp.float32
BF16 = jnp.bfloat16
VMEM_LIMIT = 56 * 1024 * 1024


def _const_spec(shape):
    return pl.BlockSpec(shape, lambda *_: (0,) * len(shape), pipeline_mode=pl.Buffered(1))


def _sigmoid(x):
    return 1.0 / (1.0 + jnp.exp(-x))


def _rms(x, g):
    return x * lax.rsqrt(jnp.mean(x * x, axis=-1, keepdims=True) + RMS_EPS) * g


def _dot(a, b):
    return jnp.dot(a.astype(BF16), b.astype(BF16), preferred_element_type=F32)


def _split_dot(x, m, terms, m_left=False):
    acc = None
    rem = x
    for _ in range(terms):
        part = rem.astype(BF16)
        d = jnp.dot(*((m, part) if m_left else (part, m)), preferred_element_type=F32)
        acc = d if acc is None else acc + d
        rem = rem - part.astype(F32)
    return acc


def _ada_kernel(cond_ref, w_ref, b_ref, o_ref):
    c = cond_ref[...]
    o_ref[...] = _dot(c * _sigmoid(c), w_ref[...]) + b_ref[...]


def _adaln(cond, w, b):
    rows = cond.shape[0]
    n = w.shape[1]
    tn = 1152
    return pl.pallas_call(
        _ada_kernel,
        grid=(n // tn,),
        in_specs=[pl.BlockSpec((rows, D_MODEL), lambda i: (0, 0)),
                  pl.BlockSpec((D_MODEL, tn), lambda i: (0, i)),
                  pl.BlockSpec((1, tn), lambda i: (0, i))],
        out_specs=pl.BlockSpec((rows, tn), lambda i: (0, i)),
        out_shape=jax.ShapeDtypeStruct((rows, n), F32),
        name="adaln",
    )(cond, w, b.reshape(1, n))


def _ffn_core(x, mod, pre_g, post_g, w_in_ref, w_out_ref):
    h = (_rms(x, pre_g) * (1.0 + mod[1:2]) + mod[0:1]).astype(BF16)
    gu = jnp.dot(h, w_in_ref[...], preferred_element_type=F32)
    gate, up = gu[:, :D_FF], gu[:, D_FF:]
    act = (gate * _sigmoid(gate) * up).astype(BF16)
    o = jnp.dot(act, w_out_ref[...], preferred_element_type=F32)
    return x + 0.5 * mod[2:3] * _rms(o, post_g)


def _ffn1_kernel(ctx_ref, x_ref, mod_ref, pre_ref, post_ref, w_in_ref, w_out_ref, o_ref):
    is_ctx = pl.program_id(1) == 0
    x = jnp.where(is_ctx, ctx_ref[0], x_ref[0])
    o_ref[0] = _ffn_core(x, mod_ref[0, 0, 0:3], pre_ref[...], post_ref[...], w_in_ref, w_out_ref)


def _ffn1(ctx, x, mods, pre_g, post_g, w_in, w_out):
    b = x.shape[0]
    nt = ALL_LEN // TM
    return pl.pallas_call(
        _ffn1_kernel,
        grid=(b, nt),
        in_specs=[pl.BlockSpec((1, TM, D_MODEL), lambda i, j: (i, 0, 0)),
                  pl.BlockSpec((1, TM, D_MODEL), lambda i, j: (i, jnp.maximum(j - 1, 0), 0)),
                  pl.BlockSpec((1, 1, N_MOD, D_MODEL), lambda i, j: (i, jnp.minimum(j, 1), 0, 0)),
                  _const_spec((1, D_MODEL)), _const_spec((1, D_MODEL)),
                  _const_spec((D_MODEL, 2 * D_FF)), _const_spec((D_FF, D_MODEL))],
        out_specs=pl.BlockSpec((1, TM, D_MODEL), lambda i, j: (i, j, 0)),
        out_shape=jax.ShapeDtypeStruct((b, ALL_LEN, D_MODEL), F32),
        compiler_params=pltpu.CompilerParams(
            dimension_semantics=("arbitrary", "arbitrary"), vmem_limit_bytes=VMEM_LIMIT),
        name="ffn1",
    )(ctx, x, mods, pre_g, post_g, w_in, w_out)


def _lane_tile(t, reps):
    return jnp.concatenate([t] * reps, axis=1)


def _mixin_kernel(x_ref, mod_ref, pre_ref, w_ref, cos_ref, sin_a_ref, sin_b_ref,
                  urw_ref, q_ref, k_ref, vt_ref, gate_ref):
    mod = mod_ref[0, 0]
    h = (_rms(x_ref[0], pre_ref[...]) * (1.0 + mod[4:5]) + mod[3:4]).astype(BF16)
    u = jnp.dot(h, w_ref[...], preferred_element_type=F32)
    urw_ref[0] = u[:, :RWKV_COLS]
    reps = DIFF_WIDTH // 128
    cos_t = _lane_tile(cos_ref[...], reps)
    sin_a = _lane_tile(sin_a_ref[...], reps)
    sin_b = _lane_tile(sin_b_ref[...], reps)

    def rope(z):
        return (z * cos_t + pltpu.roll(z, DIFF_WIDTH - ROPE_FREQS, 1) * sin_a
                + pltpu.roll(z, ROPE_FREQS, 1) * sin_b)

    c0 = RWKV_COLS
    q_ref[0] = (rope(u[:, c0:c0 + DIFF_WIDTH]) * (HEAD_DIM ** -0.5 * LOG2_E)).astype(BF16)
    k_ref[0] = rope(u[:, c0 + DIFF_WIDTH:c0 + 2 * DIFF_WIDTH]).astype(BF16)
    vt_ref[0] = u[:, c0 + 2 * DIFF_WIDTH:c0 + 3 * DIFF_WIDTH].T.astype(BF16)
    gate_ref[0] = _sigmoid(u[:, c0 + 3 * DIFF_WIDTH:]).astype(BF16)


def _mixin(x1, mods, pre_g, w, cos_t, sin_a, sin_b):
    b = x1.shape[0]
    nt = ALL_LEN // TM
    tile = lambda width: pl.BlockSpec((1, TM, width), lambda i, j: (i, j, 0))
    latent = lambda width: pl.BlockSpec((1, TM, width), lambda i, j: (i, jnp.maximum(j - 1, 0), 0))
    rope_spec = pl.BlockSpec((TM, 128), lambda i, j: (j, 0))
    return pl.pallas_call(
        _mixin_kernel,
        grid=(b, nt),
        in_specs=[tile(D_MODEL),
                  pl.BlockSpec((1, 1, N_MOD, D_MODEL), lambda i, j: (i, jnp.minimum(j, 1), 0, 0)),
                  _const_spec((1, D_MODEL)), _const_spec((D_MODEL, MIX_COLS)),
                  rope_spec, rope_spec, rope_spec],
        out_specs=[tile(RWKV_COLS), latent(DIFF_WIDTH), tile(DIFF_WIDTH),
                   pl.BlockSpec((1, DIFF_WIDTH, TM), lambda i, j: (i, 0, j)),
                   latent(2 * D_MODEL)],
        out_shape=[jax.ShapeDtypeStruct((b, ALL_LEN, RWKV_COLS), F32),
                   jax.ShapeDtypeStruct((b, SEQ, DIFF_WIDTH), BF16),
                   jax.ShapeDtypeStruct((b, ALL_LEN, DIFF_WIDTH), BF16),
                   jax.ShapeDtypeStruct((b, DIFF_WIDTH, ALL_LEN), BF16),
                   jax.ShapeDtypeStruct((b, SEQ, 2 * D_MODEL), BF16)],
        compiler_params=pltpu.CompilerParams(
            dimension_semantics=("arbitrary", "arbitrary"), vmem_limit_bytes=VMEM_LIMIT),
        name="mix_in",
    )(x1, mods, pre_g, w, cos_t, sin_a, sin_b)


def _rope_tables():
    rows = SEQ // GRID_W
    row = jnp.repeat(jnp.arange(rows, dtype=F32), GRID_W)
    col = jnp.tile(jnp.arange(GRID_W, dtype=F32), rows)
    freqs = ROPE_BASE ** (-jnp.arange(ROPE_FREQS, dtype=F32) / ROPE_FREQS)
    p = np.arange(128) % HEAD_DIM
    f = p % ROPE_FREQS
    ang = jnp.where((p // 32)[None, :] == 0, row[:, None], col[:, None]) * freqs[f][None, :]
    first = ((p % 32) < ROPE_FREQS)[None, :]
    cos_x, sin_x = jnp.cos(ang), jnp.sin(ang)
    cos_t = jnp.concatenate([jnp.ones((CTX_LEN, 128), F32), cos_x])
    sin_a = jnp.concatenate([jnp.zeros((CTX_LEN, 128), F32), jnp.where(first, -sin_x, 0.0)])
    sin_b = jnp.concatenate([jnp.zeros((CTX_LEN, 128), F32), jnp.where(first, 0.0, sin_x)])
    return cos_t, sin_a, sin_b


_M_STRICT, _M_INCL, _M_D16, _M_OFF, _M_EYE = range(5)
BPS = 4
N_GROUPS = RWKV_HEADS // GROUP
N_INST = 2 * BPS * N_GROUPS


def _rwkv_masks():
    t = np.arange(CHUNK)[:, None]
    i = np.arange(CHUNK)[None, :]
    cat = lambda m: np.tile(m.astype(np.float32), (1, GROUP))
    packs, tris = [], []
    for direction in range(2):
        strict = (i < t) if direction == 0 else (i > t)
        incl = (i <= t) if direction == 0 else (i >= t)
        d16 = (t // 16 == i // 16)
        packs.append(np.stack([cat(strict), cat(incl), cat(strict & d16), cat(strict & ~d16), cat(t == i)]))
        tris.append(np.kron(np.eye(BPS), incl).astype(np.float32))
    block = (np.arange(GW)[:, None] // HEAD_DIM == np.arange(GW)[None, :] // HEAD_DIM)
    head = (np.arange(RWKV_WIDTH)[:, None] // HEAD_DIM == np.arange(RWKV_WIDTH)[None, :] // HEAD_DIM)
    return (jnp.asarray(np.stack(packs), F32), jnp.asarray(np.stack(tris), BF16),
            jnp.asarray(block, BF16), jnp.asarray(head, BF16))


def _bmm(a, b):
    return jnp.einsum("imk,ikn->imn", a.astype(BF16), b.astype(BF16), preferred_element_type=F32)


def _bmm_nt(a, b):
    return jnp.einsum("imk,ink->imn", a.astype(BF16), b.astype(BF16), preferred_element_type=F32)


def _rwkv_chunks(state_ref, fwd, bwd, masks_ref, block):
    c = CHUNK
    inst = lambda name: jnp.stack([d[name][e * c:(e + 1) * c, g * GW:(g + 1) * GW]
                                   for d in (fwd, bwd) for e in range(BPS) for g in range(N_GROUPS)])
    per_dir = N_INST // 2
    mask = lambda m_id: jnp.stack([masks_ref[0, m_id]] * per_dir + [masks_ref[1, m_id]] * per_dir)
    bd = lambda x: jnp.concatenate([x.astype(BF16)] * GROUP, axis=1) * block
    rows = lambda *parts: jnp.concatenate(parts, axis=1)
    r, logw, cum_incl, kd, v, kk, a = map(inst, ("r", "logw", "cum", "kd", "v", "kk", "a"))
    a_t = kk * jnp.exp(cum_incl - logw)
    r_t = r * jnp.exp(cum_incl)
    e_inv = jnp.exp(-cum_incl)
    b_t = kk * a * e_inv
    k_t = kd * e_inv
    g = _bmm_nt(rows(a_t, r_t), rows(bd(b_t), bd(k_t)))
    g_ab = g[:, :c, :GW]
    l_ak = g[:, :c, GW:] * mask(_M_STRICT)
    m_rb = g[:, c:, :GW] * mask(_M_INCL)
    m_rk = g[:, c:, GW:] * mask(_M_INCL)
    n1 = -(g_ab * mask(_M_D16))
    n2 = _bmm(n1, bd(n1))
    t_d = mask(_M_EYE) + n1
    p = _bmm(rows(t_d, n2), bd(n2))
    t_d, n4 = t_d + p[:, :c], p[:, c:]
    p = _bmm(rows(t_d, n4), bd(n4))
    t_d, n8 = t_d + p[:, :c], p[:, c:]
    t_d = t_d + _bmm(t_d, bd(n8))
    m1 = -_bmm(g_ab * mask(_M_OFF), bd(t_d))
    p = _bmm(rows(t_d, m1), bd(m1))
    q, m2 = t_d + p[:, :c], p[:, c:]
    t_inv = q + _bmm(q, bd(m2))
    p = _bmm(rows(l_ak, m_rk), bd(v))
    x, y_v = p[:, :c], p[:, c:]
    state = state_ref[...]
    ar_s = _bmm_nt(rows(a_t, r_t), bd(state))
    z = _bmm(t_inv, bd(ar_s[:, :c] + x))
    y = ar_s[:, c:] + y_v - _bmm(m_rb, bd(z))
    vz = rows(v, -z)
    kb = rows(k_t, b_t)
    upd = jnp.stack([_dot(vz[i].T, kb[i]) for i in range(N_INST)])
    lane_head = lax.broadcasted_iota(jnp.int32, (1, 1, GW), 2) // HEAD_DIM
    own = upd[:, (GROUP - 1) * HEAD_DIM:]
    for h in range(GROUP - 2, -1, -1):
        own = jnp.where(lane_head == h, upd[:, h * HEAD_DIM:(h + 1) * HEAD_DIM], own)
    total = jnp.sum(logw, axis=1, keepdims=True)
    state_ref[...] = (state + own) * jnp.exp(total)
    return y


def _rwkv_streams(direction, j, u_ref, up_ref, un_ref, p, bon_ref, g_ref):
    (sw, w0, w2, a0, a2, g2, k_k, k_a, r_k, masks_ref, tri_ref, block_ref, head_ref) = p
    rows = lax.broadcasted_iota(jnp.int32, (CHUNK, 1), 0)
    has_prev = jnp.logical_and(j != 0, j != CTX_CHUNKS)
    has_next = jnp.logical_and(j != CTX_CHUNKS - 1, j != N_CHUNKS - 1)
    convs = []
    for e in range(BPS):
        u = u_ref[e]
        prev_row = jnp.where(has_prev, up_ref[e, HALO - 1:HALO, :], 0.0)
        next_row = jnp.where(has_next, un_ref[e, 0:1, :], 0.0)
        u_m1 = jnp.where(rows == 0, prev_row, pltpu.roll(u, 1, 0))
        u_p1 = jnp.where(rows == CHUNK - 1, next_row, pltpu.roll(u, CHUNK - 1, 0))
        convs.append(u_m1 * sw[0:1] + u * sw[1:2] + u_p1 * sw[2:3])
    uc = jnp.concatenate(convs, axis=0)

    wdt = RWKV_WIDTH
    r = uc[:, 0:wdt]
    k = uc[:, wdt:2 * wdt]
    v = uc[:, 2 * wdt:3 * wdt]
    c0 = 3 * wdt
    w_logit = w0[direction:direction + 1] + _dot(jnp.tanh(uc[:, c0:c0 + LORA_W]), w2[direction])
    logw = -math.exp(-0.5) * _sigmoid(w_logit)
    a = _sigmoid(a0[direction:direction + 1] + _dot(uc[:, c0 + LORA_W:c0 + 2 * LORA_W], a2[direction]))
    head = head_ref[...]
    kk = k * k_k[...]
    kk = kk * lax.rsqrt(_split_dot(kk * kk, head, 2) + 1e-12)
    kd = k * (1.0 + (a - 1.0) * k_a[...])
    bon = _split_dot(r * kd * r_k[...], head, 2) * v
    gate = _dot(_sigmoid(uc[:, c0 + 2 * LORA_W:]), g2[...]) if g_ref is not None else None
    for e in range(BPS):
        bon_ref[e] = bon[e * CHUNK:(e + 1) * CHUNK]
        if g_ref is not None:
            g_ref[e] = gate[e * CHUNK:(e + 1) * CHUNK]

    cum = _split_dot(logw, tri_ref[direction], 2, m_left=True)
    return dict(r=r, logw=logw, cum=cum, kd=kd, v=v, kk=kk, a=a)


def _bwd_chunk(s):
    return jnp.where(s < CTX_CHUNKS, CTX_CHUNKS - 1 - s, N_CHUNKS + CTX_CHUNKS - 1 - s)


def _rwkv_kernel(uf_ref, ufp_ref, ufn_ref, ub_ref, ubp_ref, ubn_ref, *rest):
    params, outs, state_ref = rest[:13], rest[13:18], rest[18]
    y0_ref, y1_ref, bon0_ref, bon1_ref, g_ref = outs
    s = pl.program_id(1)

    @pl.when(s == 0)
    def _():
        state_ref[...] = jnp.zeros_like(state_ref)

    fwd = _rwkv_streams(0, s, uf_ref, ufp_ref, ufn_ref, params, bon0_ref, g_ref)
    bwd = _rwkv_streams(1, _bwd_chunk(s), ub_ref, ubp_ref, ubn_ref, params, bon1_ref, None)
    y = _rwkv_chunks(state_ref, fwd, bwd, params[9], params[11][...])
    for d, y_ref in enumerate((y0_ref, y1_ref)):
        for e in range(BPS):
            first = (d * BPS + e) * N_GROUPS
            y_ref[e] = jnp.concatenate([y[first + g] for g in range(N_GROUPS)], axis=1)


def _rwkv(urw, sw, w0, w2, a0, a2, g2, k_k, k_a, r_k, consts):
    b = urw.shape[0]
    masks, tri, block, head = consts
    halo_blocks = ALL_LEN // HALO
    per_chunk = CHUNK // HALO
    fwd = lambda s: s
    def specs(chunk_of):
        return [pl.BlockSpec((BPS, CHUNK, RWKV_COLS), lambda i, s: (i, chunk_of(s), 0)),
                pl.BlockSpec((BPS, HALO, RWKV_COLS),
                             lambda i, s: (i, jnp.maximum(chunk_of(s) * per_chunk - 1, 0), 0)),
                pl.BlockSpec((BPS, HALO, RWKV_COLS),
                             lambda i, s: (i, jnp.minimum((chunk_of(s) + 1) * per_chunk, halo_blocks - 1), 0))]
    out_f = pl.BlockSpec((BPS, CHUNK, RWKV_WIDTH), lambda i, s: (i, s, 0))
    out_b = pl.BlockSpec((BPS, CHUNK, RWKV_WIDTH), lambda i, s: (i, _bwd_chunk(s), 0))
    shape = jax.ShapeDtypeStruct((b, ALL_LEN, RWKV_WIDTH), F32)
    param_arrays = (sw, w0, w2, a0, a2, g2, k_k, k_a, r_k, masks, tri, block, head)
    return pl.pallas_call(
        _rwkv_kernel,
        grid=(b // BPS, N_CHUNKS),
        in_specs=specs(fwd) + specs(_bwd_chunk) + [_const_spec(p.shape) for p in param_arrays],
        out_specs=[out_f, out_b, out_f, out_b, out_f],
        out_shape=[shape] * 5,
        scratch_shapes=[pltpu.VMEM((N_INST, HEAD_DIM, GW), F32)],
        compiler_params=pltpu.CompilerParams(
            dimension_semantics=("arbitrary", "arbitrary"), vmem_limit_bytes=VMEM_LIMIT),
        name="rwkv",
    )(urw, urw, urw, urw, urw, urw, *param_arrays)


def _attn_kernel(q_ref, k_ref, vt_ref, lam_ref, g_ref, o_ref):
    q = q_ref[0]
    lane = lax.broadcasted_iota(jnp.int32, (1, DIFF_V_DIM), 1)
    first = lane < HEAD_DIM
    zero = jnp.zeros_like(q)
    q2 = jnp.concatenate([jnp.where(first, q, zero), jnp.where(first, zero, q)], axis=0)
    k = k_ref[0]
    vt = vt_ref[0]
    parts = []
    n_tiles = 2 * TQ // Q_COLS
    scores = lambda c: lax.dot_general(k, q2[c * Q_COLS:(c + 1) * Q_COLS], (((1,), (1,)), ((), ())),
                                       preferred_element_type=F32)
    st_next = scores(0)
    for c in range(n_tiles):
        st = st_next
        if c + 1 < n_tiles:
            st_next = scores(c + 1)
        e = jnp.exp2(st - jnp.max(st, axis=0, keepdims=True))
        inv_l = 1.0 / jnp.sum(e, axis=0, keepdims=True)
        parts.append(jnp.dot(vt, e.astype(BF16), preferred_element_type=F32) * inv_l)
    ot = jnp.concatenate(parts, axis=1)
    lp = lam_ref[...]
    lane_sum = lambda t: jnp.sum(t, axis=-1, keepdims=True)
    lam = jnp.exp(lane_sum(lp[0:1] * lp[1:2])) - jnp.exp(lane_sum(lp[2:3] * lp[3:4])) + LAM_INIT
    o = ot[:, :TQ] - lam * ot[:, TQ:]
    o = o * lax.rsqrt(jnp.mean(o * o, axis=0, keepdims=True) + RMS_EPS) * g_ref[...] * (1.0 - LAM_INIT)
    o_ref[0] = o.T.astype(BF16)


def _attn(q, k, vt, lam_p, subln_g):
    b = q.shape[0]
    return pl.pallas_call(
        _attn_kernel,
        grid=(b, DIFF_HEADS, SEQ // TQ),
        in_specs=[pl.BlockSpec((1, TQ, DIFF_V_DIM), lambda i, h, t: (i, t, h)),
                  pl.BlockSpec((1, ALL_LEN, DIFF_V_DIM), lambda i, h, t: (i, 0, h)),
                  pl.BlockSpec((1, DIFF_V_DIM, ALL_LEN), lambda i, h, t: (i, h, 0)),
                  pl.BlockSpec((4, HEAD_DIM), lambda i, h, t: (0, 0)),
                  pl.BlockSpec((DIFF_V_DIM, 1), lambda i, h, t: (0, 0))],
        out_specs=pl.BlockSpec((1, TQ, DIFF_V_DIM), lambda i, h, t: (i, t, h)),
        out_shape=jax.ShapeDtypeStruct((b, SEQ, DIFF_WIDTH), BF16),
        compiler_params=pltpu.CompilerParams(
            dimension_semantics=("arbitrary", "arbitrary", "arbitrary"), vmem_limit_bytes=VMEM_LIMIT),
        name="diff_attn",
    )(q, k, vt, lam_p, subln_g)


def _merge_kernel(x_ref, mod_ref, y0_ref, y1_ref, bon0_ref, bon1_ref, g_ref, yb_ref, gate_ref,
                  avg_ref, ln_g_ref, ln_b_ref, up_a_ref, up_b_ref, w_out_ref, post_ref,
                  pre2_ref, post2_ref, w_in2_ref, w_out2_ref, o_ref):
    y = y0_ref[0] + y1_ref[0]
    avg = avg_ref[...]
    yc = y - _split_dot(y, avg, 2)
    var = _split_dot(yc * yc, avg, 2)
    yn = yc * lax.rsqrt(var + RWKV_GN_EPS) * ln_g_ref[...] + ln_b_ref[...]
    ya = ((yn + bon0_ref[0] + bon1_ref[0]) * g_ref[0]).astype(BF16)
    gates = gate_ref[0].astype(F32)
    mix = (gates[:, :D_MODEL] * jnp.dot(ya, up_a_ref[...], preferred_element_type=F32)
           + gates[:, D_MODEL:] * jnp.dot(yb_ref[0], up_b_ref[...], preferred_element_type=F32))
    out = jnp.dot(mix.astype(BF16), w_out_ref[...], preferred_element_type=F32)
    x2 = x_ref[0] + mod_ref[0, 0, 5:6] * _rms(out, post_ref[...])
    o_ref[0] = _ffn_core(x2, mod_ref[0, 0, 6:9], pre2_ref[...], post2_ref[...], w_in2_ref, w_out2_ref)


def _merge_ffn2(x1, mods, y0, y1, bon0, bon1, g, yb, gates, avg, ln_g, ln_b, up_a, up_b, w_out, post_g,
                pre2_g, post2_g, w_in2, w_out2):
    b = x1.shape[0]
    off = CTX_LEN // TM
    shifted = lambda width: pl.BlockSpec((1, TM, width), lambda i, j: (i, j + off, 0))
    return pl.pallas_call(
        _merge_kernel,
        grid=(b, SEQ // TM),
        in_specs=[shifted(D_MODEL),
                  pl.BlockSpec((1, 1, N_MOD, D_MODEL), lambda i, j: (i, 1, 0, 0)),
                  shifted(RWKV_WIDTH), shifted(RWKV_WIDTH), shifted(RWKV_WIDTH), shifted(RWKV_WIDTH),
                  shifted(RWKV_WIDTH),
                  pl.BlockSpec((1, TM, DIFF_WIDTH), lambda i, j: (i, j, 0)),
                  pl.BlockSpec((1, TM, 2 * D_MODEL), lambda i, j: (i, j, 0)),
                  _const_spec((RWKV_WIDTH, RWKV_WIDTH)), _const_spec((1, RWKV_WIDTH)),
                  _const_spec((1, RWKV_WIDTH)), _const_spec((RWKV_WIDTH, D_MODEL)),
                  _const_spec((DIFF_WIDTH, D_MODEL)), _const_spec((D_MODEL, D_MODEL)),
                  _const_spec((1, D_MODEL)),
                  _const_spec((1, D_MODEL)), _const_spec((1, D_MODEL)),
                  _const_spec((D_MODEL, 2 * D_FF)), _const_spec((D_FF, D_MODEL))],
        out_specs=pl.BlockSpec((1, TM, D_MODEL), lambda i, j: (i, j, 0)),
        out_shape=jax.ShapeDtypeStruct((b, SEQ, D_MODEL), F32),
        compiler_params=pltpu.CompilerParams(
            dimension_semantics=("arbitrary", "arbitrary"), vmem_limit_bytes=VMEM_LIMIT),
        name="merge_ffn2",
    )(x1, mods, y0, y1, bon0, bon1, g, yb, gates, avg, ln_g, ln_b, up_a, up_b, w_out, post_g,
      pre2_g, post2_g, w_in2, w_out2)


def kernel(x, c, ctx, c_ctx, ada_w, ada_b, pre_norm_g, post_norm_g, ffn1_w_in, ffn1_w_out, mix_w_in,
           rwkv_shift_w, rwkv_w0, rwkv_w2, rwkv_a0, rwkv_a2, rwkv_g2, rwkv_k_k, rwkv_k_a, rwkv_r_k,
           rwkv_ln_g, rwkv_ln_b, diff_lambda, diff_subln_g, branch_up_a, branch_up_b, mix_w_out,
           ffn2_w_in, ffn2_w_out):
    assert x.shape[1:] == (SEQ, D_MODEL) and ctx.shape[1:] == (CTX_LEN, D_MODEL)
    assert ada_w.shape[0] == 1, "single-layer stack"
    b = x.shape[0]
    assert b % BPS == 0
    row = lambda t: t.reshape(1, -1)

    pad = (-(b + 1)) % 8
    cond = jnp.concatenate([c, c_ctx[None], jnp.zeros((pad, D_MODEL), F32)], axis=0)
    m = _adaln(cond, ada_w[0], ada_b[0]).reshape(b + 1 + pad, N_MOD, D_MODEL)
    mods = jnp.stack([jnp.broadcast_to(m[b], (b, N_MOD, D_MODEL)), m[:b]], axis=1)

    x1 = _ffn1(ctx, x, mods, row(pre_norm_g[0, 0]), row(post_norm_g[0, 0]),
               ffn1_w_in[0].astype(BF16), ffn1_w_out[0].astype(BF16))

    urw, q, k, vt, gates = _mixin(x1, mods, row(pre_norm_g[0, 1]), mix_w_in[0].astype(BF16), *_rope_tables())

    zeros = jnp.zeros((LORA_W // 2, RWKV_WIDTH), F32)
    pad_dir = lambda w: jnp.stack([jnp.concatenate([w[0], zeros]), jnp.concatenate([zeros, w[1]])]).astype(BF16)
    consts = _rwkv_masks()
    y0, y1, bon0, bon1, g = _rwkv(
        urw, rwkv_shift_w[0], rwkv_w0[0], pad_dir(rwkv_w2[0]), rwkv_a0[0], pad_dir(rwkv_a2[0]),
        rwkv_g2[0].astype(BF16), row(rwkv_k_k[0]), row(rwkv_k_a[0]), row(rwkv_r_k[0]), consts)

    yb = _attn(q, k, vt, diff_lambda[0], diff_subln_g[0].reshape(DIFF_V_DIM, 1))

    avg = (consts[3].astype(F32) / HEAD_DIM).astype(BF16)
    return _merge_ffn2(x1, mods, y0, y1, bon0, bon1, g, yb, gates, avg, row(rwkv_ln_g[0]), row(rwkv_ln_b[0]),
                       branch_up_a[0].astype(BF16), branch_up_b[0].astype(BF16), mix_w_out[0].astype(BF16),
                       row(post_norm_g[0, 1]), row(pre_norm_g[0, 2]), row(post_norm_g[0, 2]),
                       ffn2_w_in[0].astype(BF16), ffn2_w_out[0].astype(BF16))
```

```python
import math

import numpy as np
import jax
import jax.numpy as jnp
from jax import lax
from jax.experimental import pallas as pl
from jax.experimental.pallas import tpu as pltpu

D_MODEL = 1024
SEQ = 2048
CTX_LEN = 256
ALL_LEN = CTX_LEN + SEQ
GRID_W = 64
N_MOD = 9
D_FF = 2816
RMS_EPS = 1e-6

RWKV_HEADS = 8
HEAD_DIM = 64
RWKV_WIDTH = 512
LORA_W = 128
RWKV_COLS = 1920
RWKV_GN_EPS = 64e-5

DIFF_HEADS = 4
DIFF_WIDTH = 512
DIFF_V_DIM = 128
ROPE_BASE = 10000.0
ROPE_FREQS = 16
LAM_INIT = 0.8 - 0.6 * math.exp(-0.0)

MIX_COLS = RWKV_COLS + 3 * DIFF_WIDTH + 2 * D_MODEL

TM = 256
TILES = ALL_LEN // TM
SUB = 2
CHUNK = 64
GROUP = 4
GW = GROUP * HEAD_DIM
N_CHUNKS = ALL_LEN // CHUNK
CTX_CHUNKS = CTX_LEN // CHUNK
HALO = 8
TQ = 1024
Q_COLS = 512
LOG2_E = math.log2(math.e)

F32 = jnp.float32
BF16 = jnp.bfloat16
VMEM_LIMIT = 56 * 1024 * 1024


def _const_spec(shape):
    return pl.BlockSpec(shape, lambda *_: (0,) * len(shape), pipeline_mode=pl.Buffered(1))


def _sigmoid(x):
    return 1.0 / (1.0 + jnp.exp(-x))


def _rms(x, g):
    return x * lax.rsqrt(jnp.mean(x * x, axis=-1, keepdims=True) + RMS_EPS) * g


def _dot(a, b):
    return jnp.dot(a.astype(BF16), b.astype(BF16), preferred_element_type=F32)


def _split_dot(x, m, terms, m_left=False):
    acc = None
    rem = x
    for _ in range(terms):
        part = rem.astype(BF16)
        d = jnp.dot(*((m, part) if m_left else (part, m)), preferred_element_type=F32)
        acc = d if acc is None else acc + d
        rem = rem - part.astype(F32)
    return acc


def _ada_kernel(cond_ref, w_ref, b_ref, o_ref):
    c = cond_ref[...]
    o_ref[...] = _dot(c * _sigmoid(c), w_ref[...]) + b_ref[...]


def _adaln(cond, w, b):
    rows = cond.shape[0]
    n = w.shape[1]
    tn = 1152
    return pl.pallas_call(
        _ada_kernel,
        grid=(n // tn,),
        in_specs=[pl.BlockSpec((rows, D_MODEL), lambda i: (0, 0)),
                  pl.BlockSpec((D_MODEL, tn), lambda i: (0, i)),
                  pl.BlockSpec((1, tn), lambda i: (0, i))],
        out_specs=pl.BlockSpec((rows, tn), lambda i: (0, i)),
        out_shape=jax.ShapeDtypeStruct((rows, n), F32),
        name="adaln",
    )(cond, w, b.reshape(1, n))


def _ffn_core(x, mod, pre_g, post_g, w_in_ref, w_out_ref):
    h = (_rms(x, pre_g) * (1.0 + mod[1:2]) + mod[0:1]).astype(BF16)
    gu = jnp.dot(h, w_in_ref[...], preferred_element_type=F32)
    gate, up = gu[:, :D_FF], gu[:, D_FF:]
    act = (gate * _sigmoid(gate) * up).astype(BF16)
    o = jnp.dot(act, w_out_ref[...], preferred_element_type=F32)
    return x + 0.5 * mod[2:3] * _rms(o, post_g)


def _ffn1_kernel(*refs):
    pre_ref, post_ref, w_in_ref, w_out_ref, o_ref = refs[3 * SUB:]
    for s in range(SUB):
        ctx_ref, x_ref, mod_ref = refs[3 * s:3 * s + 3]
        tile = SUB * pl.program_id(0) + s
        is_ctx = tile % TILES == 0
        x = jnp.where(is_ctx, ctx_ref[...], x_ref[...])
        o_ref[s * TM:(s + 1) * TM] = _ffn_core(x, mod_ref[0, 0, 0:3], pre_ref[...], post_ref[...],
                                               w_in_ref, w_out_ref)


def _ffn1(ctx, x, mods, pre_g, post_g, w_in, w_out):
    b = x.shape[0]
    lat_tiles = SEQ // TM
    n_tiles = b * TILES
    assert CTX_LEN == TM and n_tiles % SUB == 0
    specs = []
    for s in range(SUB):
        tile = lambda n, s=s: SUB * n + s
        specs += [pl.BlockSpec((TM, D_MODEL), lambda n, t=tile: (t(n) // TILES, 0)),
                  pl.BlockSpec((TM, D_MODEL),
                               lambda n, t=tile: (t(n) // TILES * lat_tiles + jnp.maximum(t(n) % TILES - 1, 0), 0)),
                  pl.BlockSpec((1, 1, N_MOD, D_MODEL),
                               lambda n, t=tile: (t(n) // TILES, jnp.minimum(t(n) % TILES, 1), 0, 0))]
    out = pl.pallas_call(
        _ffn1_kernel,
        grid=(n_tiles // SUB,),
        in_specs=specs + [_const_spec((1, D_MODEL)), _const_spec((1, D_MODEL)),
                          _const_spec((D_MODEL, 2 * D_FF)), _const_spec((D_FF, D_MODEL))],
        out_specs=pl.BlockSpec((SUB * TM, D_MODEL), lambda n: (n, 0)),
        out_shape=jax.ShapeDtypeStruct((b * ALL_LEN, D_MODEL), F32),
        compiler_params=pltpu.CompilerParams(
            dimension_semantics=("arbitrary",), vmem_limit_bytes=VMEM_LIMIT),
        name="ffn1",
    )(*([ctx.reshape(b * CTX_LEN, D_MODEL), x.reshape(b * SEQ, D_MODEL), mods] * SUB), pre_g, post_g, w_in, w_out)
    return out.reshape(b, ALL_LEN, D_MODEL)


def _lane_tile(t, reps):
    return jnp.concatenate([t] * reps, axis=1)


def _head_sum(x, block, terms):
    return jnp.concatenate([_split_dot(x[:, g * GW:(g + 1) * GW], block, terms) for g in range(N_GROUPS)], axis=1)


def _mixin_kernel(x_ref, xp_ref, xn_ref, mod_ref, pre_ref, w_ref, cos_ref, sin_a_ref, sin_b_ref,
                  sw_ref, w0_ref, w2_ref, a0_ref, a2_ref, g2_ref, k_k_ref, k_a_ref, r_k_ref, block_ref,
                  q_ref, k_ref, vt_ref, gate_ref, r_ref, v_ref, kk_ref, a_f_ref, a_b_ref, kd_f_ref, kd_b_ref,
                  logw_f_ref, logw_b_ref, bon_ref, g_ref):
    j = pl.program_id(1)
    mod = mod_ref[0, 0]
    xs = jnp.concatenate([xp_ref[0], x_ref[0], xn_ref[0]], axis=0)
    h = (_rms(xs, pre_ref[...]) * (1.0 + mod[4:5]) + mod[3:4]).astype(BF16)
    u_rwkv = jnp.dot(h, w_ref[:, :RWKV_COLS], preferred_element_type=F32)
    u = jnp.dot(h[HALO:HALO + TM], w_ref[:, RWKV_COLS:], preferred_element_type=F32)
    reps = DIFF_WIDTH // 128
    cos_t = _lane_tile(cos_ref[...], reps)
    sin_a = _lane_tile(sin_a_ref[...], reps)
    sin_b = _lane_tile(sin_b_ref[...], reps)

    def rope(z):
        return (z * cos_t + pltpu.roll(z, DIFF_WIDTH - ROPE_FREQS, 1) * sin_a
                + pltpu.roll(z, ROPE_FREQS, 1) * sin_b)

    q_ref[0] = (rope(u[:, :DIFF_WIDTH]) * (HEAD_DIM ** -0.5 * LOG2_E)).astype(BF16)
    k_ref[0] = rope(u[:, DIFF_WIDTH:2 * DIFF_WIDTH]).astype(BF16)
    vt_ref[0] = u[:, 2 * DIFF_WIDTH:3 * DIFF_WIDTH].T.astype(BF16)
    gate_ref[0] = _sigmoid(u[:, 3 * DIFF_WIDTH:]).astype(BF16)

    ur = u_rwkv[HALO:HALO + TM]
    rows = lax.broadcasted_iota(jnp.int32, (TM, 1), 0)
    first_latent = CTX_LEN // TM
    has_prev = jnp.logical_and(j != 0, j != first_latent)
    has_next = jnp.logical_and(j != first_latent - 1, j != TILES - 1)
    prev_row = jnp.where(has_prev, u_rwkv[HALO - 1:HALO], 0.0)
    next_row = jnp.where(has_next, u_rwkv[HALO + TM:HALO + TM + 1], 0.0)
    u_m1 = jnp.where(rows == 0, prev_row, pltpu.roll(ur, 1, 0))
    u_p1 = jnp.where(rows == TM - 1, next_row, pltpu.roll(ur, TM - 1, 0))
    uc = u_m1 * sw_ref[0:1] + ur * sw_ref[1:2] + u_p1 * sw_ref[2:3]

    wdt = RWKV_WIDTH
    r = uc[:, 0:wdt]
    k = uc[:, wdt:2 * wdt]
    v = uc[:, 2 * wdt:3 * wdt]
    l0 = 3 * wdt
    block = block_ref[...]
    kk = k * k_k_ref[...]
    kk = kk * lax.rsqrt(_head_sum(kk * kk, block, 2) + 1e-12)
    decay_in = jnp.tanh(uc[:, l0:l0 + LORA_W])
    rate_in = uc[:, l0 + LORA_W:l0 + 2 * LORA_W]
    kd_sum = None
    for d, (a_ref, kd_ref, logw_ref) in enumerate(((a_f_ref, kd_f_ref, logw_f_ref), (a_b_ref, kd_b_ref, logw_b_ref))):
        w_logit = w0_ref[d:d + 1] + _dot(decay_in, w2_ref[d])
        logw_ref[0] = -math.exp(-0.5) * _sigmoid(w_logit)
        a = _sigmoid(a0_ref[d:d + 1] + _dot(rate_in, a2_ref[d]))
        kd = k * (1.0 + (a - 1.0) * k_a_ref[...])
        a_ref[0] = a.astype(BF16)
        kd_ref[0] = kd.astype(BF16)
        kd_sum = kd if kd_sum is None else kd_sum + kd
    r_ref[0] = r.astype(BF16)
    v_ref[0] = v.astype(BF16)
    kk_ref[0] = kk.astype(BF16)
    bon_ref[0] = _head_sum(r * r_k_ref[...] * kd_sum, block, 2) * v
    g_ref[0] = _dot(_sigmoid(uc[:, l0 + 2 * LORA_W:]), g2_ref[...])


def _mixin(x1, mods, pre_g, w, cos_t, sin_a, sin_b, sw, w0, w2, a0, a2, g2, k_k, k_a, r_k, block):
    b = x1.shape[0]
    per_tile = TM // HALO
    halo_blocks = ALL_LEN // HALO
    tile = lambda width: pl.BlockSpec((1, TM, width), lambda i, j: (i, j, 0))
    latent = lambda width: pl.BlockSpec((1, TM, width), lambda i, j: (i, jnp.maximum(j - 1, 0), 0))
    rope_spec = pl.BlockSpec((TM, 128), lambda i, j: (j, 0))
    params = (sw, w0, w2, a0, a2, g2, k_k, k_a, r_k, block)
    all_tokens = lambda width, dtype: jax.ShapeDtypeStruct((b, ALL_LEN, width), dtype)
    latents = lambda width, dtype: jax.ShapeDtypeStruct((b, SEQ, width), dtype)
    wd = RWKV_WIDTH
    return pl.pallas_call(
        _mixin_kernel,
        grid=(b, TILES),
        in_specs=[tile(D_MODEL),
                  pl.BlockSpec((1, HALO, D_MODEL), lambda i, j: (i, jnp.maximum(j * per_tile - 1, 0), 0)),
                  pl.BlockSpec((1, HALO, D_MODEL),
                               lambda i, j: (i, jnp.minimum((j + 1) * per_tile, halo_blocks - 1), 0)),
                  pl.BlockSpec((1, 1, N_MOD, D_MODEL), lambda i, j: (i, jnp.minimum(j, 1), 0, 0)),
                  _const_spec((1, D_MODEL)), _const_spec((D_MODEL, MIX_COLS)),
                  rope_spec, rope_spec, rope_spec] + [_const_spec(p.shape) for p in params],
        out_specs=[latent(DIFF_WIDTH), tile(DIFF_WIDTH),
                   pl.BlockSpec((1, DIFF_WIDTH, TM), lambda i, j: (i, 0, j)),
                   latent(2 * D_MODEL)] + [tile(wd)] * 9 + [latent(wd), latent(wd)],
        out_shape=[latents(DIFF_WIDTH, BF16), all_tokens(DIFF_WIDTH, BF16),
                   jax.ShapeDtypeStruct((b, DIFF_WIDTH, ALL_LEN), BF16), latents(2 * D_MODEL, BF16)]
                  + [all_tokens(wd, BF16)] * 7 + [all_tokens(wd, F32)] * 2 + [latents(wd, F32)] * 2,
        compiler_params=pltpu.CompilerParams(
            dimension_semantics=("arbitrary", "arbitrary"), vmem_limit_bytes=VMEM_LIMIT),
        name="mix_in",
    )(x1, x1, x1, mods, pre_g, w, cos_t, sin_a, sin_b, *params)


def _rope_tables():
    rows = SEQ // GRID_W
    row = jnp.repeat(jnp.arange(rows, dtype=F32), GRID_W)
    col = jnp.tile(jnp.arange(GRID_W, dtype=F32), rows)
    freqs = ROPE_BASE ** (-jnp.arange(ROPE_FREQS, dtype=F32) / ROPE_FREQS)
    p = np.arange(128) % HEAD_DIM
    f = p % ROPE_FREQS
    ang = jnp.where((p // 32)[None, :] == 0, row[:, None], col[:, None]) * freqs[f][None, :]
    first = ((p % 32) < ROPE_FREQS)[None, :]
    cos_x, sin_x = jnp.cos(ang), jnp.sin(ang)
    cos_t = jnp.concatenate([jnp.ones((CTX_LEN, 128), F32), cos_x])
    sin_a = jnp.concatenate([jnp.zeros((CTX_LEN, 128), F32), jnp.where(first, -sin_x, 0.0)])
    sin_b = jnp.concatenate([jnp.zeros((CTX_LEN, 128), F32), jnp.where(first, 0.0, sin_x)])
    return cos_t, sin_a, sin_b


_M_STRICT, _M_INCL, _M_D16, _M_OFF, _M_EYE = range(5)
BPS = 4
N_GROUPS = RWKV_HEADS // GROUP
N_INST = 2 * BPS * N_GROUPS


def _rwkv_masks():
    t = np.arange(CHUNK)[:, None]
    i = np.arange(CHUNK)[None, :]
    cat = lambda m: np.tile(m.astype(np.float32), (1, GROUP))
    packs, tris = [], []
    for direction in range(2):
        strict = (i < t) if direction == 0 else (i > t)
        incl = (i <= t) if direction == 0 else (i >= t)
        d16 = (t // 16 == i // 16)
        packs.append(np.stack([cat(strict), cat(incl), cat(strict & d16), cat(strict & ~d16), cat(t == i)]))
        tris.append(np.kron(np.eye(BPS), incl).astype(np.float32))
    block = (np.arange(GW)[:, None] // HEAD_DIM == np.arange(GW)[None, :] // HEAD_DIM)
    return jnp.asarray(np.stack(packs), F32), jnp.asarray(np.stack(tris), BF16), jnp.asarray(block, BF16)


def _bmm(a, b):
    return jnp.einsum("imk,ikn->imn", a.astype(BF16), b.astype(BF16), preferred_element_type=F32)


def _bmm_nt(a, b):
    return jnp.einsum("imk,ink->imn", a.astype(BF16), b.astype(BF16), preferred_element_type=F32)


_STREAMS = ("r", "v", "kk", "a", "kd", "logw")


def _rwkv_kernel(*refs):
    n = len(_STREAMS)
    dirs = (dict(zip(_STREAMS, refs[:n])), dict(zip(_STREAMS, refs[n:2 * n])))
    masks_ref, tri_ref, block_ref, y0_ref, y1_ref, state_ref = refs[2 * n:]

    @pl.when(pl.program_id(1) == 0)
    def _():
        state_ref[...] = jnp.zeros_like(state_ref)

    c = CHUNK
    block = block_ref[...]

    def inst(name, values=None):
        src = values if values is not None else [d[name] for d in dirs]
        return jnp.stack([src[d][e, :, g * GW:(g + 1) * GW].astype(F32)
                          for d in range(2) for e in range(BPS) for g in range(N_GROUPS)])

    per_dir = N_INST // 2
    mask = lambda m_id: jnp.stack([masks_ref[0, m_id]] * per_dir + [masks_ref[1, m_id]] * per_dir)
    bd = lambda x: jnp.concatenate([x.astype(BF16)] * GROUP, axis=1) * block
    rows = lambda *parts: jnp.concatenate(parts, axis=1)
    r, v, kk, a, kd, logw = map(inst, _STREAMS)
    cums = [_split_dot(d["logw"][...].reshape(BPS * c, RWKV_WIDTH), tri_ref[i], 2, m_left=True)
            .reshape(BPS, c, RWKV_WIDTH) for i, d in enumerate(dirs)]
    cum_incl = inst(None, cums)
    a_t = kk * jnp.exp(cum_incl - logw)
    r_t = r * jnp.exp(cum_incl)
    e_inv = jnp.exp(-cum_incl)
    b_t = kk * a * e_inv
    k_t = kd * e_inv
    g = _bmm_nt(rows(a_t, r_t), rows(bd(b_t), bd(k_t)))
    g_ab = g[:, :c, :GW]
    l_ak = g[:, :c, GW:] * mask(_M_STRICT)
    m_rb = g[:, c:, :GW] * mask(_M_INCL)
    m_rk = g[:, c:, GW:] * mask(_M_INCL)
    n1 = -(g_ab * mask(_M_D16))
    n2 = _bmm(n1, bd(n1))
    t_d = mask(_M_EYE) + n1
    p = _bmm(rows(t_d, n2), bd(n2))
    t_d, n4 = t_d + p[:, :c], p[:, c:]
    p = _bmm(rows(t_d, n4), bd(n4))
    t_d, n8 = t_d + p[:, :c], p[:, c:]
    t_d = t_d + _bmm(t_d, bd(n8))
    m1 = -_bmm(g_ab * mask(_M_OFF), bd(t_d))
    p = _bmm(rows(t_d, m1), bd(m1))
    q, m2 = t_d + p[:, :c], p[:, c:]
    t_inv = q + _bmm(q, bd(m2))
    p = _bmm(rows(l_ak, m_rk), bd(v))
    x, y_v = p[:, :c], p[:, c:]
    state = state_ref[...]
    ar_s = _bmm_nt(rows(a_t, r_t), bd(state))
    z = _bmm(t_inv, bd(ar_s[:, :c] + x))
    y = ar_s[:, c:] + y_v - _bmm(m_rb, bd(z))
    vz = rows(v, -z)
    kb = rows(k_t, b_t)
    upd = jnp.stack([_dot(vz[i].T, kb[i]) for i in range(N_INST)])
    lane_head = lax.broadcasted_iota(jnp.int32, (1, 1, GW), 2) // HEAD_DIM
    own = upd[:, (GROUP - 1) * HEAD_DIM:]
    for h in range(GROUP - 2, -1, -1):
        own = jnp.where(lane_head == h, upd[:, h * HEAD_DIM:(h + 1) * HEAD_DIM], own)
    total = jnp.sum(logw, axis=1, keepdims=True)
    state_ref[...] = (state + own) * jnp.exp(total)
    for d, y_ref in enumerate((y0_ref, y1_ref)):
        for e in range(BPS):
            first = (d * BPS + e) * N_GROUPS
            y_ref[e] = jnp.concatenate([y[first + g] for g in range(N_GROUPS)], axis=1)


def _bwd_chunk(s):
    return jnp.where(s < CTX_CHUNKS, CTX_CHUNKS - 1 - s, N_CHUNKS + CTX_CHUNKS - 1 - s)


def _rwkv(fwd_streams, bwd_streams, masks, tri, block):
    b = fwd_streams[0].shape[0]
    spec_f = pl.BlockSpec((BPS, CHUNK, RWKV_WIDTH), lambda i, s: (i, s, 0))
    spec_b = pl.BlockSpec((BPS, CHUNK, RWKV_WIDTH), lambda i, s: (i, _bwd_chunk(s), 0))
    shape = jax.ShapeDtypeStruct((b, ALL_LEN, RWKV_WIDTH), F32)
    consts = (masks, tri, block)
    return pl.pallas_call(
        _rwkv_kernel,
        grid=(b // BPS, N_CHUNKS),
        in_specs=[spec_f] * len(fwd_streams) + [spec_b] * len(bwd_streams) + [_const_spec(p.shape) for p in consts],
        out_specs=[spec_f, spec_b],
        out_shape=[shape] * 2,
        scratch_shapes=[pltpu.VMEM((N_INST, HEAD_DIM, GW), F32)],
        compiler_params=pltpu.CompilerParams(
            dimension_semantics=("arbitrary", "arbitrary"), vmem_limit_bytes=VMEM_LIMIT),
        name="rwkv",
    )(*fwd_streams, *bwd_streams, *consts)


def _attn_kernel(q_ref, k_ref, vt_ref, lam_ref, g_ref, o_ref):
    q = q_ref[0]
    lane = lax.broadcasted_iota(jnp.int32, (1, DIFF_V_DIM), 1)
    first = lane < HEAD_DIM
    zero = jnp.zeros_like(q)
    q2 = jnp.concatenate([jnp.where(first, q, zero), jnp.where(first, zero, q)], axis=0)
    k = k_ref[0]
    vt = vt_ref[0]
    parts = []
    n_tiles = 2 * TQ // Q_COLS
    scores = lambda c: lax.dot_general(k, q2[c * Q_COLS:(c + 1) * Q_COLS], (((1,), (1,)), ((), ())),
                                       preferred_element_type=F32)
    st_next = scores(0)
    for c in range(n_tiles):
        st = st_next
        if c + 1 < n_tiles:
            st_next = scores(c + 1)
        e = jnp.exp2(st - jnp.max(st, axis=0, keepdims=True))
        inv_l = 1.0 / jnp.sum(e, axis=0, keepdims=True)
        parts.append(jnp.dot(vt, e.astype(BF16), preferred_element_type=F32) * inv_l)
    ot = jnp.concatenate(parts, axis=1)
    lp = lam_ref[...]
    lane_sum = lambda t: jnp.sum(t, axis=-1, keepdims=True)
    lam = jnp.exp(lane_sum(lp[0:1] * lp[1:2])) - jnp.exp(lane_sum(lp[2:3] * lp[3:4])) + LAM_INIT
    o = ot[:, :TQ] - lam * ot[:, TQ:]
    o = o * lax.rsqrt(jnp.mean(o * o, axis=0, keepdims=True) + RMS_EPS) * g_ref[...] * (1.0 - LAM_INIT)
    o_ref[0] = o.T.astype(BF16)


def _attn(q, k, vt, lam_p, subln_g):
    b = q.shape[0]
    return pl.pallas_call(
        _attn_kernel,
        grid=(b, DIFF_HEADS, SEQ // TQ),
        in_specs=[pl.BlockSpec((1, TQ, DIFF_V_DIM), lambda i, h, t: (i, t, h)),
                  pl.BlockSpec((1, ALL_LEN, DIFF_V_DIM), lambda i, h, t: (i, 0, h)),
                  pl.BlockSpec((1, DIFF_V_DIM, ALL_LEN), lambda i, h, t: (i, h, 0)),
                  pl.BlockSpec((4, HEAD_DIM), lambda i, h, t: (0, 0)),
                  pl.BlockSpec((DIFF_V_DIM, 1), lambda i, h, t: (0, 0))],
        out_specs=pl.BlockSpec((1, TQ, DIFF_V_DIM), lambda i, h, t: (i, t, h)),
        out_shape=jax.ShapeDtypeStruct((b, SEQ, DIFF_WIDTH), BF16),
        compiler_params=pltpu.CompilerParams(
            dimension_semantics=("arbitrary", "arbitrary", "arbitrary"), vmem_limit_bytes=VMEM_LIMIT),
        name="diff_attn",
    )(q, k, vt, lam_p, subln_g)


def _merge_kernel(*refs):
    combined = refs[:3 * SUB]
    (mod_ref, bon_ref, g_ref, yb_ref, gate_ref, avg_ref, ln_g_ref, ln_b_ref, up_a_ref, up_b_ref, w_out_ref,
     post_ref, pre2_ref, post2_ref, w_in2_ref, w_out2_ref, o_ref) = refs[3 * SUB:]
    avg = avg_ref[...]
    for s in range(SUB):
        x_ref, y0_ref, y1_ref = combined[3 * s:3 * s + 3]
        rows = slice(s * TM, (s + 1) * TM)
        y = y0_ref[0] + y1_ref[0]
        yc = y - _split_dot(y, avg, 2)
        var = _split_dot(yc * yc, avg, 2)
        yn = yc * lax.rsqrt(var + RWKV_GN_EPS) * ln_g_ref[...] + ln_b_ref[...]
        ya = ((yn + bon_ref[0, rows]) * g_ref[0, rows]).astype(BF16)
        gates = gate_ref[0, rows].astype(F32)
        mix = (gates[:, :D_MODEL] * jnp.dot(ya, up_a_ref[...], preferred_element_type=F32)
               + gates[:, D_MODEL:] * jnp.dot(yb_ref[0, rows], up_b_ref[...], preferred_element_type=F32))
        out = jnp.dot(mix.astype(BF16), w_out_ref[...], preferred_element_type=F32)
        x2 = x_ref[0] + mod_ref[0, 0, 5:6] * _rms(out, post_ref[...])
        o_ref[0, rows] = _ffn_core(x2, mod_ref[0, 0, 6:9], pre2_ref[...], post2_ref[...], w_in2_ref, w_out2_ref)


def _merge_ffn2(x1, mods, y0, y1, bon, g, yb, gates, avg, ln_g, ln_b, up_a, up_b, w_out, post_g,
                pre2_g, post2_g, w_in2, w_out2):
    b = x1.shape[0]
    off = CTX_LEN // TM
    assert (SEQ // TM) % SUB == 0
    combined_specs = []
    for s in range(SUB):
        for width in (D_MODEL, RWKV_WIDTH, RWKV_WIDTH):
            combined_specs.append(pl.BlockSpec((1, TM, width), lambda i, j, s=s: (i, SUB * j + s + off, 0)))
    plain = lambda width: pl.BlockSpec((1, SUB * TM, width), lambda i, j: (i, j, 0))
    return pl.pallas_call(
        _merge_kernel,
        grid=(b, SEQ // (SUB * TM)),
        in_specs=combined_specs + [
            pl.BlockSpec((1, 1, N_MOD, D_MODEL), lambda i, j: (i, 1, 0, 0)),
            plain(RWKV_WIDTH), plain(RWKV_WIDTH), plain(DIFF_WIDTH), plain(2 * D_MODEL),
            _const_spec((RWKV_WIDTH, RWKV_WIDTH)), _const_spec((1, RWKV_WIDTH)),
            _const_spec((1, RWKV_WIDTH)), _const_spec((RWKV_WIDTH, D_MODEL)),
            _const_spec((DIFF_WIDTH, D_MODEL)), _const_spec((D_MODEL, D_MODEL)),
            _const_spec((1, D_MODEL)),
            _const_spec((1, D_MODEL)), _const_spec((1, D_MODEL)),
            _const_spec((D_MODEL, 2 * D_FF)), _const_spec((D_FF, D_MODEL))],
        out_specs=plain(D_MODEL),
        out_shape=jax.ShapeDtypeStruct((b, SEQ, D_MODEL), F32),
        compiler_params=pltpu.CompilerParams(
            dimension_semantics=("arbitrary", "arbitrary"), vmem_limit_bytes=VMEM_LIMIT),
        name="merge_ffn2",
    )(*([x1, y0, y1] * SUB), mods, bon, g, yb, gates, avg, ln_g, ln_b, up_a, up_b, w_out, post_g,
      pre2_g, post2_g, w_in2, w_out2)


def kernel(x, c, ctx, c_ctx, ada_w, ada_b, pre_norm_g, post_norm_g, ffn1_w_in, ffn1_w_out, mix_w_in,
           rwkv_shift_w, rwkv_w0, rwkv_w2, rwkv_a0, rwkv_a2, rwkv_g2, rwkv_k_k, rwkv_k_a, rwkv_r_k,
           rwkv_ln_g, rwkv_ln_b, diff_lambda, diff_subln_g, branch_up_a, branch_up_b, mix_w_out,
           ffn2_w_in, ffn2_w_out):
    assert x.shape[1:] == (SEQ, D_MODEL) and ctx.shape[1:] == (CTX_LEN, D_MODEL)
    assert ada_w.shape[0] == 1, "single-layer stack"
    b = x.shape[0]
    assert b % BPS == 0
    row = lambda t: t.reshape(1, -1)

    pad = (-(b + 1)) % 8
    cond = jnp.concatenate([c, c_ctx[None], jnp.zeros((pad, D_MODEL), F32)], axis=0)
    m = _adaln(cond, ada_w[0], ada_b[0]).reshape(b + 1 + pad, N_MOD, D_MODEL)
    mods = jnp.stack([jnp.broadcast_to(m[b], (b, N_MOD, D_MODEL)), m[:b]], axis=1)

    x1 = _ffn1(ctx, x, mods, row(pre_norm_g[0, 0]), row(post_norm_g[0, 0]),
               ffn1_w_in[0].astype(BF16), ffn1_w_out[0].astype(BF16))

    zeros = jnp.zeros((LORA_W // 2, RWKV_WIDTH), F32)
    pad_dir = lambda w: jnp.stack([jnp.concatenate([w[0], zeros]), jnp.concatenate([zeros, w[1]])]).astype(BF16)
    masks, tri, block = _rwkv_masks()
    (q, k, vt, gates, r, v, kk, a_f, a_b, kd_f, kd_b, logw_f, logw_b, bon, g) = _mixin(
        x1, mods, row(pre_norm_g[0, 1]), mix_w_in[0].astype(BF16), *_rope_tables(),
        rwkv_shift_w[0], rwkv_w0[0], pad_dir(rwkv_w2[0]), rwkv_a0[0], pad_dir(rwkv_a2[0]),
        rwkv_g2[0].astype(BF16), row(rwkv_k_k[0]), row(rwkv_k_a[0]), row(rwkv_r_k[0]), block)

    y0, y1 = _rwkv((r, v, kk, a_f, kd_f, logw_f), (r, v, kk, a_b, kd_b, logw_b), masks, tri, block)

    yb = _attn(q, k, vt, diff_lambda[0], diff_subln_g[0].reshape(DIFF_V_DIM, 1))

    head = np.arange(RWKV_WIDTH) // HEAD_DIM
    avg = jnp.asarray((head[:, None] == head[None, :]) / HEAD_DIM, BF16)
    return _merge_ffn2(x1, mods, y0, y1, bon, g, yb, gates, avg, row(rwkv_ln_g[0]), row(rwkv_ln_b[0]),
                       branch_up_a[0].astype(BF16), branch_up_b[0].astype(BF16), mix_w_out[0].astype(BF16),
                       row(post_norm_g[0, 1]), row(pre_norm_g[0, 2]), row(post_norm_g[0, 2]),
                       ffn2_w_in[0].astype(BF16), ffn2_w_out[0].astype(BF16))
```

```python
import math

import numpy as np
import jax
import jax.numpy as jnp
from jax import lax
from jax.experimental import pallas as pl
from jax.experimental.pallas import tpu as pltpu

D_MODEL = 1024
SEQ = 2048
CTX_LEN = 256
ALL_LEN = CTX_LEN + SEQ
GRID_W = 64
N_MOD = 9
D_FF = 2816
RMS_EPS = 1e-6

RWKV_HEADS = 8
HEAD_DIM = 64
RWKV_WIDTH = 512
LORA_W = 128
RWKV_COLS = 1920
RWKV_GN_EPS = 64e-5

DIFF_HEADS = 4
DIFF_WIDTH = 512
DIFF_V_DIM = 128
ROPE_BASE = 10000.0
ROPE_FREQS = 16
LAM_INIT = 0.8 - 0.6 * math.exp(-0.0)

MIX_COLS = RWKV_COLS + 3 * DIFF_WIDTH + 2 * D_MODEL

TM = 256
TILES = ALL_LEN // TM
SUB = 2
CHUNK = 64
GROUP = 4
GW = GROUP * HEAD_DIM
N_CHUNKS = ALL_LEN // CHUNK
CTX_CHUNKS = CTX_LEN // CHUNK
HALO = 8
TQ = 1024
Q_COLS = 512
LOG2_E = math.log2(math.e)

F32 = jnp.float32
BF16 = jnp.bfloat16
VMEM_LIMIT = 56 * 1024 * 1024


def _const_spec(shape):
    return pl.BlockSpec(shape, lambda *_: (0,) * len(shape), pipeline_mode=pl.Buffered(1))


def _sigmoid(x):
    return 1.0 / (1.0 + jnp.exp(-x))


def _rms(x, g):
    return x * lax.rsqrt(jnp.mean(x * x, axis=-1, keepdims=True) + RMS_EPS) * g


def _dot(a, b):
    return jnp.dot(a.astype(BF16), b.astype(BF16), preferred_element_type=F32)


def _split_dot(x, m, terms, m_left=False):
    acc = None
    rem = x
    for _ in range(terms):
        part = rem.astype(BF16)
        d = jnp.dot(*((m, part) if m_left else (part, m)), preferred_element_type=F32)
        acc = d if acc is None else acc + d
        rem = rem - part.astype(F32)
    return acc


def _ada_kernel(cond_ref, w_ref, b_ref, o_ref):
    c = cond_ref[...]
    o_ref[...] = _dot(c * _sigmoid(c), w_ref[...]) + b_ref[...]


def _adaln(cond, w, b):
    rows = cond.shape[0]
    n = w.shape[1]
    tn = 1152
    return pl.pallas_call(
        _ada_kernel,
        grid=(n // tn,),
        in_specs=[pl.BlockSpec((rows, D_MODEL), lambda i: (0, 0)),
                  pl.BlockSpec((D_MODEL, tn), lambda i: (0, i)),
                  pl.BlockSpec((1, tn), lambda i: (0, i))],
        out_specs=pl.BlockSpec((rows, tn), lambda i: (0, i)),
        out_shape=jax.ShapeDtypeStruct((rows, n), F32),
        name="adaln",
    )(cond, w, b.reshape(1, n))


def _ffn_core(x, mod, pre_g, post_g, w_in_ref, w_out_ref):
    h = (_rms(x, pre_g) * (1.0 + mod[1:2]) + mod[0:1]).astype(BF16)
    gu = jnp.dot(h, w_in_ref[...], preferred_element_type=F32)
    gate, up = gu[:, :D_FF], gu[:, D_FF:]
    act = (gate * _sigmoid(gate) * up).astype(BF16)
    o = jnp.dot(act, w_out_ref[...], preferred_element_type=F32)
    return x + 0.5 * mod[2:3] * _rms(o, post_g)


def _mod_row(tile, batch):
    return jnp.where(tile % TILES == 0, batch, tile // TILES)


def _ffn1_kernel(*refs):
    pre_ref, post_ref, w_in_ref, w_out_ref, o_ref = refs[3 * SUB:]
    for s in range(SUB):
        ctx_ref, x_ref, mod_ref = refs[3 * s:3 * s + 3]
        tile = SUB * pl.program_id(0) + s
        is_ctx = tile % TILES == 0
        x = jnp.where(is_ctx, ctx_ref[...], x_ref[...])
        o_ref[s * TM:(s + 1) * TM] = _ffn_core(x, mod_ref[0, 0:3], pre_ref[...], post_ref[...],
                                               w_in_ref, w_out_ref)


def _ffn1(ctx, x, mods, pre_g, post_g, w_in, w_out):
    b = x.shape[0]
    lat_tiles = SEQ // TM
    n_tiles = b * TILES
    assert CTX_LEN == TM and n_tiles % SUB == 0
    specs = []
    for s in range(SUB):
        tile = lambda n, s=s: SUB * n + s
        specs += [pl.BlockSpec((TM, D_MODEL), lambda n, t=tile: (t(n) // TILES, 0)),
                  pl.BlockSpec((TM, D_MODEL),
                               lambda n, t=tile: (t(n) // TILES * lat_tiles + jnp.maximum(t(n) % TILES - 1, 0), 0)),
                  pl.BlockSpec((1, N_MOD, D_MODEL), lambda n, t=tile: (_mod_row(t(n), b), 0, 0))]
    out = pl.pallas_call(
        _ffn1_kernel,
        grid=(n_tiles // SUB,),
        in_specs=specs + [_const_spec((1, D_MODEL)), _const_spec((1, D_MODEL)),
                          _const_spec((D_MODEL, 2 * D_FF)), _const_spec((D_FF, D_MODEL))],
        out_specs=pl.BlockSpec((SUB * TM, D_MODEL), lambda n: (n, 0)),
        out_shape=jax.ShapeDtypeStruct((b * ALL_LEN, D_MODEL), F32),
        compiler_params=pltpu.CompilerParams(
            dimension_semantics=("arbitrary",), vmem_limit_bytes=VMEM_LIMIT),
        name="ffn1",
    )(*([ctx.reshape(b * CTX_LEN, D_MODEL), x.reshape(b * SEQ, D_MODEL), mods] * SUB), pre_g, post_g, w_in, w_out)
    return out


def _lane_tile(t, reps):
    return jnp.concatenate([t] * reps, axis=1)


def _head_sum(x, block, terms):
    return jnp.concatenate([_split_dot(x[:, g * GW:(g + 1) * GW], block, terms) for g in range(N_GROUPS)], axis=1)


def _mixin_kernel(x_ref, xp_ref, xn_ref, mod_a_ref, mod_b_ref, rope_a_ref, rope_b_ref, pre_ref, w_ref,
                  sw_ref, w0_ref, w2_ref, a0_ref, a2_ref, g2_ref, k_k_ref, k_a_ref, r_k_ref, block_ref,
                  q_ref, k_ref, vt_ref, gate_ref, r_ref, v_ref, kk_ref, a_f_ref, a_b_ref, kd_f_ref, kd_b_ref,
                  logw_f_ref, logw_b_ref, bon_ref, g_ref):
    pre = pre_ref[...]
    block = block_ref[...]
    first_latent = CTX_LEN // TM
    reps = DIFF_WIDTH // 128
    halos = ((xp_ref[...], x_ref[TM:TM + HALO]), (x_ref[TM - HALO:TM], xn_ref[...]))
    for s, (mod_ref, rope_ref) in enumerate(((mod_a_ref, rope_a_ref), (mod_b_ref, rope_b_ref))):
        tile = slice(s * TM, (s + 1) * TM)
        pos = (SUB * pl.program_id(0) + s) % TILES
        xs = jnp.concatenate([halos[s][0], x_ref[tile], halos[s][1]], axis=0)
        h = (_rms(xs, pre) * (1.0 + mod_ref[0, 4:5]) + mod_ref[0, 3:4]).astype(BF16)
        u_rwkv = jnp.dot(h, w_ref[:, :RWKV_COLS], preferred_element_type=F32)
        u = jnp.dot(h[HALO:HALO + TM], w_ref[:, RWKV_COLS:], preferred_element_type=F32)
        cos_t, sin_a, sin_b = (_lane_tile(rope_ref[i], reps) for i in range(3))

        def rope(z):
            return (z * cos_t + pltpu.roll(z, DIFF_WIDTH - ROPE_FREQS, 1) * sin_a
                    + pltpu.roll(z, ROPE_FREQS, 1) * sin_b)

        q_ref[tile] = (rope(u[:, :DIFF_WIDTH]) * (HEAD_DIM ** -0.5 * LOG2_E)).astype(BF16)
        k_ref[tile] = rope(u[:, DIFF_WIDTH:2 * DIFF_WIDTH]).astype(BF16)
        vt_ref[:, tile] = u[:, 2 * DIFF_WIDTH:3 * DIFF_WIDTH].T.astype(BF16)
        gate_ref[tile] = _sigmoid(u[:, 3 * DIFF_WIDTH:]).astype(BF16)

        ur = u_rwkv[HALO:HALO + TM]
        rows = lax.broadcasted_iota(jnp.int32, (TM, 1), 0)
        has_prev = jnp.logical_and(pos != 0, pos != first_latent)
        has_next = jnp.logical_and(pos != first_latent - 1, pos != TILES - 1)
        prev_row = jnp.where(has_prev, u_rwkv[HALO - 1:HALO], 0.0)
        next_row = jnp.where(has_next, u_rwkv[HALO + TM:HALO + TM + 1], 0.0)
        u_m1 = jnp.where(rows == 0, prev_row, pltpu.roll(ur, 1, 0))
        u_p1 = jnp.where(rows == TM - 1, next_row, pltpu.roll(ur, TM - 1, 0))
        uc = u_m1 * sw_ref[0:1] + ur * sw_ref[1:2] + u_p1 * sw_ref[2:3]

        wdt = RWKV_WIDTH
        r = uc[:, 0:wdt]
        k = uc[:, wdt:2 * wdt]
        v = uc[:, 2 * wdt:3 * wdt]
        l0 = 3 * wdt
        kk = k * k_k_ref[...]
        kk = kk * lax.rsqrt(_head_sum(kk * kk, block, 2) + 1e-12)
        decay_in = jnp.tanh(uc[:, l0:l0 + LORA_W])
        rate_in = uc[:, l0 + LORA_W:l0 + 2 * LORA_W]
        kd_sum = None
        for d, (a_ref, kd_ref, logw_ref) in enumerate(((a_f_ref, kd_f_ref, logw_f_ref),
                                                       (a_b_ref, kd_b_ref, logw_b_ref))):
            w_logit = w0_ref[d:d + 1] + _dot(decay_in, w2_ref[d])
            logw_ref[tile] = -math.exp(-0.5) * _sigmoid(w_logit)
            a = _sigmoid(a0_ref[d:d + 1] + _dot(rate_in, a2_ref[d]))
            kd = k * (1.0 + (a - 1.0) * k_a_ref[...])
            a_ref[tile] = a.astype(BF16)
            kd_ref[tile] = kd.astype(BF16)
            kd_sum = kd if kd_sum is None else kd_sum + kd
        r_ref[tile] = r.astype(BF16)
        v_ref[tile] = v.astype(BF16)
        kk_ref[tile] = kk.astype(BF16)
        bon_ref[tile] = _head_sum(r * r_k_ref[...] * kd_sum, block, 2) * v
        g_ref[tile] = _dot(_sigmoid(uc[:, l0 + 2 * LORA_W:]), g2_ref[...])


def _mixin(x1, m3, pre_g, w, rope, sw, w0, w2, a0, a2, g2, k_k, k_a, r_k, block):
    n_rows = x1.shape[0]
    b = n_rows // ALL_LEN
    rows_all = SUB * TM
    assert SUB == 2 and (b * TILES) % SUB == 0
    per_block = rows_all // HALO
    halo_blocks = n_rows // HALO
    block_rows = lambda width: pl.BlockSpec((rows_all, width), lambda n: (n, 0))
    params = (sw, w0, w2, a0, a2, g2, k_k, k_a, r_k, block)
    flat = lambda width, dtype: jax.ShapeDtypeStruct((n_rows, width), dtype)
    wd = RWKV_WIDTH
    per_tile = lambda spec_of: [spec_of(lambda n, s=s: SUB * n + s) for s in range(SUB)]
    return pl.pallas_call(
        _mixin_kernel,
        grid=(n_rows // rows_all,),
        in_specs=[block_rows(D_MODEL),
                  pl.BlockSpec((HALO, D_MODEL), lambda n: (jnp.maximum(n * per_block - 1, 0), 0)),
                  pl.BlockSpec((HALO, D_MODEL), lambda n: (jnp.minimum((n + 1) * per_block, halo_blocks - 1), 0))]
        + per_tile(lambda t: pl.BlockSpec((1, N_MOD, D_MODEL), lambda n: (_mod_row(t(n), b), 0, 0)))
        + per_tile(lambda t: pl.BlockSpec((3, TM, 128), lambda n: (0, t(n) % TILES, 0)))
        + [_const_spec((1, D_MODEL)), _const_spec((D_MODEL, MIX_COLS))] + [_const_spec(p.shape) for p in params],
        out_specs=[block_rows(DIFF_WIDTH), block_rows(DIFF_WIDTH),
                   pl.BlockSpec((DIFF_WIDTH, rows_all), lambda n: (0, n)),
                   block_rows(2 * D_MODEL)] + [block_rows(wd)] * 11,
        out_shape=[flat(DIFF_WIDTH, BF16), flat(DIFF_WIDTH, BF16),
                   jax.ShapeDtypeStruct((DIFF_WIDTH, n_rows), BF16), flat(2 * D_MODEL, BF16)]
                  + [flat(wd, BF16)] * 7 + [flat(wd, F32)] * 4,
        compiler_params=pltpu.CompilerParams(
            dimension_semantics=("arbitrary",), vmem_limit_bytes=VMEM_LIMIT),
        name="mix_in",
    )(x1, x1, x1, m3, m3, rope, rope, pre_g, w, *params)


def _rope_tables():
    rows = SEQ // GRID_W
    row = jnp.repeat(jnp.arange(rows, dtype=F32), GRID_W)
    col = jnp.tile(jnp.arange(GRID_W, dtype=F32), rows)
    freqs = ROPE_BASE ** (-jnp.arange(ROPE_FREQS, dtype=F32) / ROPE_FREQS)
    p = np.arange(128) % HEAD_DIM
    f = p % ROPE_FREQS
    ang = jnp.where((p // 32)[None, :] == 0, row[:, None], col[:, None]) * freqs[f][None, :]
    first = ((p % 32) < ROPE_FREQS)[None, :]
    cos_x, sin_x = jnp.cos(ang), jnp.sin(ang)
    cos_t = jnp.concatenate([jnp.ones((CTX_LEN, 128), F32), cos_x])
    sin_a = jnp.concatenate([jnp.zeros((CTX_LEN, 128), F32), jnp.where(first, -sin_x, 0.0)])
    sin_b = jnp.concatenate([jnp.zeros((CTX_LEN, 128), F32), jnp.where(first, 0.0, sin_x)])
    return jnp.stack([cos_t, sin_a, sin_b])


_M_STRICT, _M_INCL, _M_D16, _M_OFF, _M_EYE = range(5)
BPS = 4
N_GROUPS = RWKV_HEADS // GROUP
N_INST = 2 * BPS * N_GROUPS


def _rwkv_masks():
    t = np.arange(CHUNK)[:, None]
    i = np.arange(CHUNK)[None, :]
    cat = lambda m: np.tile(m.astype(np.float32), (1, GROUP))
    packs, tris = [], []
    for direction in range(2):
        strict = (i < t) if direction == 0 else (i > t)
        incl = (i <= t) if direction == 0 else (i >= t)
        d16 = (t // 16 == i // 16)
        packs.append(np.stack([cat(strict), cat(incl), cat(strict & d16), cat(strict & ~d16), cat(t == i)]))
        tris.append(np.kron(np.eye(BPS), incl).astype(np.float32))
    block = (np.arange(GW)[:, None] // HEAD_DIM == np.arange(GW)[None, :] // HEAD_DIM)
    return jnp.asarray(np.stack(packs), F32), jnp.asarray(np.stack(tris), BF16), jnp.asarray(block, BF16)


def _bmm(a, b):
    return jnp.einsum("imk,ikn->imn", a.astype(BF16), b.astype(BF16), preferred_element_type=F32)


def _bmm_nt(a, b):
    return jnp.einsum("imk,ink->imn", a.astype(BF16), b.astype(BF16), preferred_element_type=F32)


_STREAMS = ("r", "v", "kk", "a", "kd", "logw")


def _rwkv_kernel(*refs):
    n = len(_STREAMS)
    dirs = (dict(zip(_STREAMS, refs[:n])), dict(zip(_STREAMS, refs[n:2 * n])))
    masks_ref, tri_ref, y0_ref, y1_ref, state_ref = refs[2 * n:]

    @pl.when(pl.program_id(1) == 0)
    def _():
        state_ref[...] = jnp.zeros_like(state_ref)

    c = CHUNK

    def inst(name, values=None):
        src = values if values is not None else [d[name] for d in dirs]
        return jnp.stack([src[d][e, :, g * GW:(g + 1) * GW].astype(F32)
                          for d in range(2) for e in range(BPS) for g in range(N_GROUPS)])

    per_dir = N_INST // 2
    mask = lambda m_id: jnp.stack([masks_ref[0, m_id]] * per_dir + [masks_ref[1, m_id]] * per_dir)
    own_block = (lax.broadcasted_iota(jnp.int32, (1, GW, GW), 1) // HEAD_DIM
                 == lax.broadcasted_iota(jnp.int32, (1, GW, GW), 2) // HEAD_DIM)
    bd = lambda x: jnp.where(own_block, jnp.concatenate([x.astype(BF16)] * GROUP, axis=1), jnp.zeros((), BF16))
    rows = lambda *parts: jnp.concatenate(parts, axis=1)
    r, v, kk, a, kd, logw = map(inst, _STREAMS)
    cums = [_split_dot(d["logw"][...].reshape(BPS * c, RWKV_WIDTH), tri_ref[i], 2, m_left=True)
            .reshape(BPS, c, RWKV_WIDTH) for i, d in enumerate(dirs)]
    cum_incl = inst(None, cums)
    a_t = kk * jnp.exp(cum_incl - logw)
    r_t = r * jnp.exp(cum_incl)
    e_inv = jnp.exp(-cum_incl)
    b_t = kk * a * e_inv
    k_t = kd * e_inv
    g = _bmm_nt(rows(a_t, r_t), rows(bd(b_t), bd(k_t)))
    g_ab = g[:, :c, :GW]
    l_ak = g[:, :c, GW:] * mask(_M_STRICT)
    m_rb = g[:, c:, :GW] * mask(_M_INCL)
    m_rk = g[:, c:, GW:] * mask(_M_INCL)
    n1 = -(g_ab * mask(_M_D16))
    n2 = _bmm(n1, bd(n1))
    t_d = mask(_M_EYE) + n1
    p = _bmm(rows(t_d, n2), bd(n2))
    t_d, n4 = t_d + p[:, :c], p[:, c:]
    p = _bmm(rows(t_d, n4), bd(n4))
    t_d, n8 = t_d + p[:, :c], p[:, c:]
    t_d = t_d + _bmm(t_d, bd(n8))
    m1 = -_bmm(g_ab * mask(_M_OFF), bd(t_d))
    p = _bmm(rows(t_d, m1), bd(m1))
    q, m2 = t_d + p[:, :c], p[:, c:]
    t_inv = q + _bmm(q, bd(m2))
    p = _bmm(rows(l_ak, m_rk), bd(v))
    x, y_v = p[:, :c], p[:, c:]
    state = state_ref[...]
    ar_s = _bmm_nt(rows(a_t, r_t), bd(state))
    z = _bmm(t_inv, bd(ar_s[:, :c] + x))
    y = ar_s[:, c:] + y_v - _bmm(m_rb, bd(z))
    vz = rows(v, -z)
    kb = rows(k_t, b_t)
    upd = jnp.stack([_dot(vz[i].T, kb[i]) for i in range(N_INST)])
    lane_head = lax.broadcasted_iota(jnp.int32, (1, 1, GW), 2) // HEAD_DIM
    own = upd[:, (GROUP - 1) * HEAD_DIM:]
    for h in range(GROUP - 2, -1, -1):
        own = jnp.where(lane_head == h, upd[:, h * HEAD_DIM:(h + 1) * HEAD_DIM], own)
    total = jnp.sum(logw, axis=1, keepdims=True)
    state_ref[...] = (state + own) * jnp.exp(total)
    for d, y_ref in enumerate((y0_ref, y1_ref)):
        for e in range(BPS):
            first = (d * BPS + e) * N_GROUPS
            y_ref[e] = jnp.concatenate([y[first + g] for g in range(N_GROUPS)], axis=1)


def _bwd_chunk(s):
    return jnp.where(s < CTX_CHUNKS, CTX_CHUNKS - 1 - s, N_CHUNKS + CTX_CHUNKS - 1 - s)


def _rwkv(fwd_streams, bwd_streams, masks, tri):
    b = fwd_streams[0].shape[0]
    spec_f = pl.BlockSpec((BPS, CHUNK, RWKV_WIDTH), lambda i, s: (i, s, 0))
    spec_b = pl.BlockSpec((BPS, CHUNK, RWKV_WIDTH), lambda i, s: (i, _bwd_chunk(s), 0))
    shape = jax.ShapeDtypeStruct((b, ALL_LEN, RWKV_WIDTH), F32)
    consts = (masks, tri)
    return pl.pallas_call(
        _rwkv_kernel,
        grid=(b // BPS, N_CHUNKS),
        in_specs=[spec_f] * len(fwd_streams) + [spec_b] * len(bwd_streams) + [_const_spec(p.shape) for p in consts],
        out_specs=[spec_f, spec_b],
        out_shape=[shape] * 2,
        scratch_shapes=[pltpu.VMEM((N_INST, HEAD_DIM, GW), F32)],
        compiler_params=pltpu.CompilerParams(
            dimension_semantics=("arbitrary", "arbitrary"), vmem_limit_bytes=VMEM_LIMIT),
        name="rwkv",
    )(*fwd_streams, *bwd_streams, *consts)


def _attn_kernel(q_ref, k_ref, vt_ref, lam_ref, g_ref, o_ref):
    q = q_ref[0, pl.ds(pl.multiple_of(CTX_LEN + pl.program_id(2) * TQ, TM), TQ)]
    lane = lax.broadcasted_iota(jnp.int32, (1, DIFF_V_DIM), 1)
    first = lane < HEAD_DIM
    zero = jnp.zeros_like(q)
    q2 = jnp.concatenate([jnp.where(first, q, zero), jnp.where(first, zero, q)], axis=0)
    k = k_ref[0]
    vt = vt_ref[...]
    parts = []
    n_tiles = 2 * TQ // Q_COLS
    scores = lambda c: lax.dot_general(k, q2[c * Q_COLS:(c + 1) * Q_COLS], (((1,), (1,)), ((), ())),
                                       preferred_element_type=F32)
    st_next = scores(0)
    for c in range(n_tiles):
        st = st_next
        if c + 1 < n_tiles:
            st_next = scores(c + 1)
        e = jnp.exp2(st - jnp.max(st, axis=0, keepdims=True))
        inv_l = 1.0 / jnp.sum(e, axis=0, keepdims=True)
        parts.append(jnp.dot(vt, e.astype(BF16), preferred_element_type=F32) * inv_l)
    ot = jnp.concatenate(parts, axis=1)
    lp = lam_ref[...]
    lane_sum = lambda t: jnp.sum(t, axis=-1, keepdims=True)
    lam = jnp.exp(lane_sum(lp[0:1] * lp[1:2])) - jnp.exp(lane_sum(lp[2:3] * lp[3:4])) + LAM_INIT
    o = ot[:, :TQ] - lam * ot[:, TQ:]
    o = o * lax.rsqrt(jnp.mean(o * o, axis=0, keepdims=True) + RMS_EPS) * g_ref[...] * (1.0 - LAM_INIT)
    o_ref[0] = o.T.astype(BF16)


def _attn(q, k, vt, lam_p, subln_g):
    b = q.shape[0]
    return pl.pallas_call(
        _attn_kernel,
        grid=(b, DIFF_HEADS, SEQ // TQ),
        in_specs=[pl.BlockSpec((1, ALL_LEN, DIFF_V_DIM), lambda i, h, t: (i, 0, h)),
                  pl.BlockSpec((1, ALL_LEN, DIFF_V_DIM), lambda i, h, t: (i, 0, h)),
                  pl.BlockSpec((DIFF_V_DIM, ALL_LEN), lambda i, h, t: (h, i)),
                  pl.BlockSpec((4, HEAD_DIM), lambda i, h, t: (0, 0)),
                  pl.BlockSpec((DIFF_V_DIM, 1), lambda i, h, t: (0, 0))],
        out_specs=pl.BlockSpec((1, TQ, DIFF_V_DIM), lambda i, h, t: (i, t, h)),
        out_shape=jax.ShapeDtypeStruct((b, SEQ, DIFF_WIDTH), BF16),
        compiler_params=pltpu.CompilerParams(
            dimension_semantics=("arbitrary", "arbitrary", "arbitrary"), vmem_limit_bytes=VMEM_LIMIT),
        name="diff_attn",
    )(q, k, vt, lam_p, subln_g)


_COMBINED = (D_MODEL, RWKV_WIDTH, RWKV_WIDTH, RWKV_WIDTH, RWKV_WIDTH, 2 * D_MODEL)


def _merge_kernel(*refs):
    n_comb = len(_COMBINED)
    combined = refs[:n_comb * SUB]
    (mod_ref, yb_ref, avg_ref, ln_g_ref, ln_b_ref, up_a_ref, up_b_ref, w_out_ref,
     post_ref, pre2_ref, post2_ref, w_in2_ref, w_out2_ref, o_ref) = refs[n_comb * SUB:]
    avg = avg_ref[...]
    for s in range(SUB):
        x_ref, y0_ref, y1_ref, bon_ref, g_ref, gate_ref = combined[n_comb * s:n_comb * (s + 1)]
        rows = slice(s * TM, (s + 1) * TM)
        y = y0_ref[0] + y1_ref[0]
        yc = y - _split_dot(y, avg, 2)
        var = _split_dot(yc * yc, avg, 2)
        yn = yc * lax.rsqrt(var + RWKV_GN_EPS) * ln_g_ref[...] + ln_b_ref[...]
        ya = ((yn + bon_ref[0]) * g_ref[0]).astype(BF16)
        gates = gate_ref[0].astype(F32)
        mix = (gates[:, :D_MODEL] * jnp.dot(ya, up_a_ref[...], preferred_element_type=F32)
               + gates[:, D_MODEL:] * jnp.dot(yb_ref[0, rows], up_b_ref[...], preferred_element_type=F32))
        out = jnp.dot(mix.astype(BF16), w_out_ref[...], preferred_element_type=F32)
        x2 = x_ref[0] + mod_ref[0, 5:6] * _rms(out, post_ref[...])
        o_ref[0, rows] = _ffn_core(x2, mod_ref[0, 6:9], pre2_ref[...], post2_ref[...], w_in2_ref, w_out2_ref)


def _merge_ffn2(x1, m3, y0, y1, bon, g, gates, yb, avg, ln_g, ln_b, up_a, up_b, w_out, post_g,
                pre2_g, post2_g, w_in2, w_out2):
    b = x1.shape[0]
    off = CTX_LEN // TM
    assert (SEQ // TM) % SUB == 0
    combined_specs = []
    for s in range(SUB):
        for width in _COMBINED:
            combined_specs.append(pl.BlockSpec((1, TM, width), lambda i, j, s=s: (i, SUB * j + s + off, 0)))
    plain = lambda width: pl.BlockSpec((1, SUB * TM, width), lambda i, j: (i, j, 0))
    return pl.pallas_call(
        _merge_kernel,
        grid=(b, SEQ // (SUB * TM)),
        in_specs=combined_specs + [
            pl.BlockSpec((1, N_MOD, D_MODEL), lambda i, j: (i, 0, 0)),
            plain(DIFF_WIDTH),
            _const_spec((RWKV_WIDTH, RWKV_WIDTH)), _const_spec((1, RWKV_WIDTH)),
            _const_spec((1, RWKV_WIDTH)), _const_spec((RWKV_WIDTH, D_MODEL)),
            _const_spec((DIFF_WIDTH, D_MODEL)), _const_spec((D_MODEL, D_MODEL)),
            _const_spec((1, D_MODEL)),
            _const_spec((1, D_MODEL)), _const_spec((1, D_MODEL)),
            _const_spec((D_MODEL, 2 * D_FF)), _const_spec((D_FF, D_MODEL))],
        out_specs=plain(D_MODEL),
        out_shape=jax.ShapeDtypeStruct((b, SEQ, D_MODEL), F32),
        compiler_params=pltpu.CompilerParams(
            dimension_semantics=("arbitrary", "arbitrary"), vmem_limit_bytes=VMEM_LIMIT),
        name="merge_ffn2",
    )(*([x1, y0, y1, bon, g, gates] * SUB), m3, yb, avg, ln_g, ln_b, up_a, up_b, w_out, post_g,
      pre2_g, post2_g, w_in2, w_out2)


def kernel(x, c, ctx, c_ctx, ada_w, ada_b, pre_norm_g, post_norm_g, ffn1_w_in, ffn1_w_out, mix_w_in,
           rwkv_shift_w, rwkv_w0, rwkv_w2, rwkv_a0, rwkv_a2, rwkv_g2, rwkv_k_k, rwkv_k_a, rwkv_r_k,
           rwkv_ln_g, rwkv_ln_b, diff_lambda, diff_subln_g, branch_up_a, branch_up_b, mix_w_out,
           ffn2_w_in, ffn2_w_out):
    assert x.shape[1:] == (SEQ, D_MODEL) and ctx.shape[1:] == (CTX_LEN, D_MODEL)
    assert ada_w.shape[0] == 1, "single-layer stack"
    b = x.shape[0]
    assert b % BPS == 0
    row = lambda t: t.reshape(1, -1)

    pad = (-(b + 1)) % 8
    cond = jnp.concatenate([c, c_ctx[None], jnp.zeros((pad, D_MODEL), F32)], axis=0)
    m3 = _adaln(cond, ada_w[0], ada_b[0]).reshape(b + 1 + pad, N_MOD, D_MODEL)

    x1 = _ffn1(ctx, x, m3, row(pre_norm_g[0, 0]), row(post_norm_g[0, 0]),
               ffn1_w_in[0].astype(BF16), ffn1_w_out[0].astype(BF16))

    zeros = jnp.zeros((LORA_W // 2, RWKV_WIDTH), F32)
    pad_dir = lambda w: jnp.stack([jnp.concatenate([w[0], zeros]), jnp.concatenate([zeros, w[1]])]).astype(BF16)
    masks, tri, block = _rwkv_masks()
    outs = _mixin(x1, m3, row(pre_norm_g[0, 1]), mix_w_in[0].astype(BF16), _rope_tables(),
                  rwkv_shift_w[0], rwkv_w0[0], pad_dir(rwkv_w2[0]), rwkv_a0[0], pad_dir(rwkv_a2[0]),
                  rwkv_g2[0].astype(BF16), row(rwkv_k_k[0]), row(rwkv_k_a[0]), row(rwkv_r_k[0]), block)
    vt = outs[2]
    per_batch = lambda t: t.reshape(b, ALL_LEN, t.shape[-1])
    q, k, gates, r, v, kk, a_f, a_b, kd_f, kd_b, logw_f, logw_b, bon, g = map(per_batch, outs[:2] + outs[3:])

    y0, y1 = _rwkv((r, v, kk, a_f, kd_f, logw_f), (r, v, kk, a_b, kd_b, logw_b), masks, tri)

    yb = _attn(q, k, vt, diff_lambda[0], diff_subln_g[0].reshape(DIFF_V_DIM, 1))

    head = np.arange(RWKV_WIDTH) // HEAD_DIM
    avg = jnp.asarray((head[:, None] == head[None, :]) / HEAD_DIM, BF16)
    return _merge_ffn2(per_batch(x1), m3, y0, y1, bon, g, gates, yb, avg, row(rwkv_ln_g[0]), row(rwkv_ln_b[0]),
                       branch_up_a[0].astype(BF16), branch_up_b[0].astype(BF16), mix_w_out[0].astype(BF16),
                       row(post_norm_g[0, 1]), row(pre_norm_g[0, 2]), row(post_norm_g[0, 2]),
                       ffn2_w_in[0].astype(BF16), ffn2_w_out[0].astype(BF16))
```

```python
import math

import numpy as np
import jax
import jax.numpy as jnp
from jax import lax
from jax.experimental import pallas as pl
from jax.experimental.pallas import tpu as pltpu

D_MODEL = 1024
SEQ = 2048
CTX_LEN = 256
ALL_LEN = CTX_LEN + SEQ
GRID_W = 64
N_MOD = 9
D_FF = 2816
RMS_EPS = 1e-6

RWKV_HEADS = 8
HEAD_DIM = 64
RWKV_WIDTH = 512
LORA_W = 128
RWKV_COLS = 1920
RWKV_GN_EPS = 64e-5

DIFF_HEADS = 4
DIFF_WIDTH = 512
DIFF_V_DIM = 128
ROPE_BASE = 10000.0
ROPE_FREQS = 16
LAM_INIT = 0.8 - 0.6 * math.exp(-0.0)

MIX_COLS = RWKV_COLS + 3 * DIFF_WIDTH + 2 * D_MODEL

TM = 256
TILES = ALL_LEN // TM
SUB = 2
CHUNK = 64
GROUP = 4
GW = GROUP * HEAD_DIM
N_CHUNKS = ALL_LEN // CHUNK
CTX_CHUNKS = CTX_LEN // CHUNK
HALO = 8
TQ = 1024
Q_COLS = 512
LOG2_E = math.log2(math.e)

F32 = jnp.float32
BF16 = jnp.bfloat16
VMEM_LIMIT = 56 * 1024 * 1024


def _const_spec(shape):
    return pl.BlockSpec(shape, lambda *_: (0,) * len(shape), pipeline_mode=pl.Buffered(1))


def _sigmoid(x):
    return 1.0 / (1.0 + jnp.exp(-x))


def _rms(x, g):
    return x * lax.rsqrt(jnp.mean(x * x, axis=-1, keepdims=True) + RMS_EPS) * g


def _dot(a, b):
    return jnp.dot(a.astype(BF16), b.astype(BF16), preferred_element_type=F32)


def _split_dot(x, m, terms, m_left=False):
    acc = None
    rem = x
    for _ in range(terms):
        part = rem.astype(BF16)
        d = jnp.dot(*((m, part) if m_left else (part, m)), preferred_element_type=F32)
        acc = d if acc is None else acc + d
        rem = rem - part.astype(F32)
    return acc


def _ada_kernel(cond_ref, w_ref, b_ref, o_ref):
    c = cond_ref[...]
    o_ref[...] = _dot(c * _sigmoid(c), w_ref[...]) + b_ref[...]


def _adaln(cond, w, b):
    rows = cond.shape[0]
    n = w.shape[1]
    tn = 1152
    return pl.pallas_call(
        _ada_kernel,
        grid=(n // tn,),
        in_specs=[pl.BlockSpec((rows, D_MODEL), lambda i: (0, 0)),
                  pl.BlockSpec((D_MODEL, tn), lambda i: (0, i)),
                  pl.BlockSpec((1, tn), lambda i: (0, i))],
        out_specs=pl.BlockSpec((rows, tn), lambda i: (0, i)),
        out_shape=jax.ShapeDtypeStruct((rows, n), F32),
        name="adaln",
    )(cond, w, b.reshape(1, n))


def _ffn_core(x, mod, pre_g, post_g, w_in_ref, w_out_ref):
    h = (_rms(x, pre_g) * (1.0 + mod[1:2]) + mod[0:1]).astype(BF16)
    gu = jnp.dot(h, w_in_ref[...], preferred_element_type=F32)
    gate, up = gu[:, :D_FF], gu[:, D_FF:]
    act = (gate * _sigmoid(gate) * up).astype(BF16)
    o = jnp.dot(act, w_out_ref[...], preferred_element_type=F32)
    return x + 0.5 * mod[2:3] * _rms(o, post_g)


def _mod_row(tile, batch):
    return jnp.where(tile % TILES == 0, batch, tile // TILES)


def _ffn1_kernel(*refs):
    pre_ref, post_ref, w_in_ref, w_out_ref, o_ref = refs[3 * SUB:]
    for s in range(SUB):
        ctx_ref, x_ref, mod_ref = refs[3 * s:3 * s + 3]
        tile = SUB * pl.program_id(0) + s
        is_ctx = tile % TILES == 0
        x = jnp.where(is_ctx, ctx_ref[...], x_ref[...])
        o_ref[s * TM:(s + 1) * TM] = _ffn_core(x, mod_ref[0, 0:3], pre_ref[...], post_ref[...],
                                               w_in_ref, w_out_ref)


def _ffn1(ctx, x, mods, pre_g, post_g, w_in, w_out):
    b = x.shape[0]
    lat_tiles = SEQ // TM
    n_tiles = b * TILES
    assert CTX_LEN == TM and n_tiles % SUB == 0
    specs = []
    for s in range(SUB):
        tile = lambda n, s=s: SUB * n + s
        specs += [pl.BlockSpec((TM, D_MODEL), lambda n, t=tile: (t(n) // TILES, 0)),
                  pl.BlockSpec((TM, D_MODEL),
                               lambda n, t=tile: (t(n) // TILES * lat_tiles + jnp.maximum(t(n) % TILES - 1, 0), 0)),
                  pl.BlockSpec((1, N_MOD, D_MODEL), lambda n, t=tile: (_mod_row(t(n), b), 0, 0))]
    out = pl.pallas_call(
        _ffn1_kernel,
        grid=(n_tiles // SUB,),
        in_specs=specs + [_const_spec((1, D_MODEL)), _const_spec((1, D_MODEL)),
                          _const_spec((D_MODEL, 2 * D_FF)), _const_spec((D_FF, D_MODEL))],
        out_specs=pl.BlockSpec((SUB * TM, D_MODEL), lambda n: (n, 0)),
        out_shape=jax.ShapeDtypeStruct((b * ALL_LEN, D_MODEL), F32),
        compiler_params=pltpu.CompilerParams(
            dimension_semantics=("arbitrary",), vmem_limit_bytes=VMEM_LIMIT),
        name="ffn1",
    )(*([ctx.reshape(b * CTX_LEN, D_MODEL), x.reshape(b * SEQ, D_MODEL), mods] * SUB), pre_g, post_g, w_in, w_out)
    return out


def _lane_tile(t, reps):
    return jnp.concatenate([t] * reps, axis=1)


def _head_sum(x, block, terms):
    return jnp.concatenate([_split_dot(x[:, g * GW:(g + 1) * GW], block, terms) for g in range(N_GROUPS)], axis=1)


def _mixin_kernel(x_ref, xp_ref, xn_ref, mod_a_ref, mod_b_ref, rope_a_ref, rope_b_ref, pre_ref, w_ref,
                  sw_ref, w0_ref, w2_ref, a0_ref, a2_ref, g2_ref, k_k_ref, k_a_ref, r_k_ref, block_ref,
                  q_ref, k_ref, vt_ref, gate_ref, r_ref, v_ref, kk_ref, a_f_ref, a_b_ref, kd_f_ref, kd_b_ref,
                  logw_f_ref, logw_b_ref, bon_ref, g_ref):
    pre = pre_ref[...]
    block = block_ref[...]
    first_latent = CTX_LEN // TM
    reps = DIFF_WIDTH // 128
    halos = ((xp_ref[...], x_ref[TM:TM + HALO]), (x_ref[TM - HALO:TM], xn_ref[...]))
    for s, (mod_ref, rope_ref) in enumerate(((mod_a_ref, rope_a_ref), (mod_b_ref, rope_b_ref))):
        tile = slice(s * TM, (s + 1) * TM)
        pos = (SUB * pl.program_id(0) + s) % TILES
        xs = jnp.concatenate([halos[s][0], x_ref[tile], halos[s][1]], axis=0)
        h = (_rms(xs, pre) * (1.0 + mod_ref[0, 4:5]) + mod_ref[0, 3:4]).astype(BF16)
        u_rwkv = jnp.dot(h, w_ref[:, :RWKV_COLS], preferred_element_type=F32)
        u = jnp.dot(h[HALO:HALO + TM], w_ref[:, RWKV_COLS:], preferred_element_type=F32)
        cos_t, sin_a, sin_b = (_lane_tile(rope_ref[i], reps) for i in range(3))

        def rope(z):
            return (z * cos_t + pltpu.roll(z, DIFF_WIDTH - ROPE_FREQS, 1) * sin_a
                    + pltpu.roll(z, ROPE_FREQS, 1) * sin_b)

        q_ref[tile] = (rope(u[:, :DIFF_WIDTH]) * (HEAD_DIM ** -0.5 * LOG2_E)).astype(BF16)
        k_ref[tile] = rope(u[:, DIFF_WIDTH:2 * DIFF_WIDTH]).astype(BF16)
        vt_ref[:, tile] = u[:, 2 * DIFF_WIDTH:3 * DIFF_WIDTH].T.astype(BF16)
        gate_ref[tile] = _sigmoid(u[:, 3 * DIFF_WIDTH:]).astype(BF16)

        ur = u_rwkv[HALO:HALO + TM]
        rows = lax.broadcasted_iota(jnp.int32, (TM, 1), 0)
        has_prev = jnp.logical_and(pos != 0, pos != first_latent)
        has_next = jnp.logical_and(pos != first_latent - 1, pos != TILES - 1)
        prev_row = jnp.where(has_prev, u_rwkv[HALO - 1:HALO], 0.0)
        next_row = jnp.where(has_next, u_rwkv[HALO + TM:HALO + TM + 1], 0.0)
        u_m1 = jnp.where(rows == 0, prev_row, pltpu.roll(ur, 1, 0))
        u_p1 = jnp.where(rows == TM - 1, next_row, pltpu.roll(ur, TM - 1, 0))
        uc = u_m1 * sw_ref[0:1] + ur * sw_ref[1:2] + u_p1 * sw_ref[2:3]

        wdt = RWKV_WIDTH
        r = uc[:, 0:wdt]
        k = uc[:, wdt:2 * wdt]
        v = uc[:, 2 * wdt:3 * wdt]
        l0 = 3 * wdt
        kk = k * k_k_ref[...]
        kk = kk * lax.rsqrt(_head_sum(kk * kk, block, 1) + 1e-12)
        decay_in = jnp.tanh(uc[:, l0:l0 + LORA_W])
        rate_in = uc[:, l0 + LORA_W:l0 + 2 * LORA_W]
        kd_sum = None
        for d, (a_ref, kd_ref, logw_ref) in enumerate(((a_f_ref, kd_f_ref, logw_f_ref),
                                                       (a_b_ref, kd_b_ref, logw_b_ref))):
            w_logit = w0_ref[d:d + 1] + _dot(decay_in, w2_ref[d])
            logw_ref[tile] = -math.exp(-0.5) * _sigmoid(w_logit)
            a = _sigmoid(a0_ref[d:d + 1] + _dot(rate_in, a2_ref[d]))
            kd = k * (1.0 + (a - 1.0) * k_a_ref[...])
            a_ref[tile] = a
            kd_ref[tile] = kd
            kd_sum = kd if kd_sum is None else kd_sum + kd
        r_ref[tile] = r
        v_ref[tile] = v.astype(BF16)
        kk_ref[tile] = kk
        bon_ref[tile] = _head_sum(r * r_k_ref[...] * kd_sum, block, 1) * v
        g_ref[tile] = _dot(_sigmoid(uc[:, l0 + 2 * LORA_W:]), g2_ref[...])


_STREAM_OUTS = ("r", "v", "kk", "a_f", "a_b", "kd_f", "kd_b", "logw_f", "logw_b", "bon", "g")


def _mixin(x1, m3, pre_g, w, rope, sw, w0, w2, a0, a2, g2, k_k, k_a, r_k, block):
    n_rows = x1.shape[0]
    b = n_rows // ALL_LEN
    rows_all = SUB * TM
    assert SUB == 2 and (b * TILES) % SUB == 0
    per_block = rows_all // HALO
    halo_blocks = n_rows // HALO
    block_rows = lambda width: pl.BlockSpec((rows_all, width), lambda n: (n, 0))
    params = (sw, w0, w2, a0, a2, g2, k_k, k_a, r_k, block)
    flat = lambda width, dtype: jax.ShapeDtypeStruct((n_rows, width), dtype)
    wd = RWKV_WIDTH
    per_tile = lambda spec_of: [spec_of(lambda n, s=s: SUB * n + s) for s in range(SUB)]
    return pl.pallas_call(
        _mixin_kernel,
        grid=(n_rows // rows_all,),
        in_specs=[block_rows(D_MODEL),
                  pl.BlockSpec((HALO, D_MODEL), lambda n: (jnp.maximum(n * per_block - 1, 0), 0)),
                  pl.BlockSpec((HALO, D_MODEL), lambda n: (jnp.minimum((n + 1) * per_block, halo_blocks - 1), 0))]
        + per_tile(lambda t: pl.BlockSpec((1, N_MOD, D_MODEL), lambda n: (_mod_row(t(n), b), 0, 0)))
        + per_tile(lambda t: pl.BlockSpec((3, TM, 128), lambda n: (0, t(n) % TILES, 0)))
        + [_const_spec((1, D_MODEL)), _const_spec((D_MODEL, MIX_COLS))] + [_const_spec(p.shape) for p in params],
        out_specs=[block_rows(DIFF_WIDTH), block_rows(DIFF_WIDTH),
                   pl.BlockSpec((DIFF_WIDTH, rows_all), lambda n: (0, n)),
                   block_rows(2 * D_MODEL)] + [block_rows(wd)] * len(_STREAM_OUTS),
        out_shape=[flat(DIFF_WIDTH, BF16), flat(DIFF_WIDTH, BF16),
                   jax.ShapeDtypeStruct((DIFF_WIDTH, n_rows), BF16), flat(2 * D_MODEL, BF16)]
                  + [flat(wd, BF16 if name == "v" else F32) for name in _STREAM_OUTS],
        compiler_params=pltpu.CompilerParams(
            dimension_semantics=("arbitrary",), vmem_limit_bytes=VMEM_LIMIT),
        name="mix_in",
    )(x1, x1, x1, m3, m3, rope, rope, pre_g, w, *params)


def _rope_tables():
    rows = SEQ // GRID_W
    row = jnp.repeat(jnp.arange(rows, dtype=F32), GRID_W)
    col = jnp.tile(jnp.arange(GRID_W, dtype=F32), rows)
    freqs = ROPE_BASE ** (-jnp.arange(ROPE_FREQS, dtype=F32) / ROPE_FREQS)
    p = np.arange(128) % HEAD_DIM
    f = p % ROPE_FREQS
    ang = jnp.where((p // 32)[None, :] == 0, row[:, None], col[:, None]) * freqs[f][None, :]
    first = ((p % 32) < ROPE_FREQS)[None, :]
    cos_x, sin_x = jnp.cos(ang), jnp.sin(ang)
    cos_t = jnp.concatenate([jnp.ones((CTX_LEN, 128), F32), cos_x])
    sin_a = jnp.concatenate([jnp.zeros((CTX_LEN, 128), F32), jnp.where(first, -sin_x, 0.0)])
    sin_b = jnp.concatenate([jnp.zeros((CTX_LEN, 128), F32), jnp.where(first, 0.0, sin_x)])
    return jnp.stack([cos_t, sin_a, sin_b])


_M_STRICT, _M_INCL, _M_DIAG, _M_OFF, _M_EYE = range(5)
INV_BLOCK = 8
BPS = 4
N_GROUPS = RWKV_HEADS // GROUP
N_INST = 2 * BPS * N_GROUPS


def _rwkv_masks():
    t = np.arange(CHUNK)[:, None]
    i = np.arange(CHUNK)[None, :]
    cat = lambda m: np.tile(m.astype(np.float32), (1, GROUP))
    packs, tris = [], []
    for direction in range(2):
        strict = (i < t) if direction == 0 else (i > t)
        incl = (i <= t) if direction == 0 else (i >= t)
        diag = (t // INV_BLOCK == i // INV_BLOCK)
        packs.append(np.stack([cat(strict), cat(incl), cat(strict & diag), cat(strict & ~diag), cat(t == i)]))
        tris.append(np.kron(np.eye(BPS), incl).astype(np.float32))
    block = (np.arange(GW)[:, None] // HEAD_DIM == np.arange(GW)[None, :] // HEAD_DIM)
    return jnp.asarray(np.stack(packs), F32), jnp.asarray(np.stack(tris), BF16), jnp.asarray(block, BF16)


def _bmm(a, b):
    return jnp.einsum("imk,ikn->imn", a.astype(BF16), b.astype(BF16), preferred_element_type=F32)


def _bmm_nt(a, b):
    return jnp.einsum("imk,ink->imn", a.astype(BF16), b.astype(BF16), preferred_element_type=F32)


_STREAMS = ("r", "v", "kk", "a", "kd", "logw")


def _rwkv_kernel(*refs):
    n = len(_STREAMS)
    dirs = (dict(zip(_STREAMS, refs[:n])), dict(zip(_STREAMS, refs[n:2 * n])))
    masks_ref, tri_ref, y0_ref, y1_ref, state_ref = refs[2 * n:]

    @pl.when(pl.program_id(1) == 0)
    def _():
        state_ref[...] = jnp.zeros_like(state_ref)

    c = CHUNK

    def inst(name, values=None):
        src = values if values is not None else [d[name] for d in dirs]
        return jnp.stack([src[d][e, :, g * GW:(g + 1) * GW].astype(F32)
                          for d in range(2) for e in range(BPS) for g in range(N_GROUPS)])

    per_dir = N_INST // 2
    mask = lambda m_id: jnp.stack([masks_ref[0, m_id]] * per_dir + [masks_ref[1, m_id]] * per_dir)
    own_block = (lax.broadcasted_iota(jnp.int32, (1, GW, GW), 1) // HEAD_DIM
                 == lax.broadcasted_iota(jnp.int32, (1, GW, GW), 2) // HEAD_DIM)
    bd = lambda x: jnp.where(own_block, jnp.concatenate([x.astype(BF16)] * GROUP, axis=1), jnp.zeros((), BF16))
    rows = lambda *parts: jnp.concatenate(parts, axis=1)
    r, v, kk, a, kd, logw = map(inst, _STREAMS)
    cums = [_split_dot(d["logw"][...].reshape(BPS * c, RWKV_WIDTH), tri_ref[i], 2, m_left=True)
            .reshape(BPS, c, RWKV_WIDTH) for i, d in enumerate(dirs)]
    cum_incl = inst(None, cums)
    a_t = kk * jnp.exp(cum_incl - logw)
    r_t = r * jnp.exp(cum_incl)
    e_inv = jnp.exp(-cum_incl)
    b_t = kk * a * e_inv
    k_t = kd * e_inv
    g = _bmm_nt(rows(a_t, r_t), rows(bd(b_t), bd(k_t)))
    g_ab = g[:, :c, :GW]
    l_ak = g[:, :c, GW:] * mask(_M_STRICT)
    m_rb = g[:, c:, :GW] * mask(_M_INCL)
    m_rk = g[:, c:, GW:] * mask(_M_INCL)
    n1 = -(g_ab * mask(_M_DIAG))
    n2 = _bmm(n1, bd(n1))
    t_d = mask(_M_EYE) + n1
    p = _bmm(rows(t_d, n2), bd(n2))
    t_d, n4 = t_d + p[:, :c], p[:, c:]
    t_d = t_d + _bmm(t_d, bd(n4))
    m1 = -_bmm(g_ab * mask(_M_OFF), bd(t_d))
    p = _bmm(rows(t_d, m1), bd(m1))
    q, m2 = t_d + p[:, :c], p[:, c:]
    p = _bmm(rows(q, m2), bd(m2))
    q, m4 = q + p[:, :c], p[:, c:]
    t_inv = q + _bmm(q, bd(m4))
    p = _bmm(rows(l_ak, m_rk), bd(v))
    x, y_v = p[:, :c], p[:, c:]
    state = state_ref[...]
    ar_s = _bmm_nt(rows(a_t, r_t), bd(state))
    z = _bmm(t_inv, bd(ar_s[:, :c] + x))
    y = ar_s[:, c:] + y_v - _bmm(m_rb, bd(z))
    vz = rows(v, -z)
    kb = rows(k_t, b_t)
    upd = jnp.stack([_dot(vz[i].T, kb[i]) for i in range(N_INST)])
    lane_head = lax.broadcasted_iota(jnp.int32, (1, 1, GW), 2) // HEAD_DIM
    own = upd[:, (GROUP - 1) * HEAD_DIM:]
    for h in range(GROUP - 2, -1, -1):
        own = jnp.where(lane_head == h, upd[:, h * HEAD_DIM:(h + 1) * HEAD_DIM], own)
    total = jnp.sum(logw, axis=1, keepdims=True)
    state_ref[...] = (state + own) * jnp.exp(total)
    for d, y_ref in enumerate((y0_ref, y1_ref)):
        for e in range(BPS):
            first = (d * BPS + e) * N_GROUPS
            y_ref[e] = jnp.concatenate([y[first + g] for g in range(N_GROUPS)], axis=1)


def _bwd_chunk(s):
    return jnp.where(s < CTX_CHUNKS, CTX_CHUNKS - 1 - s, N_CHUNKS + CTX_CHUNKS - 1 - s)


def _rwkv(fwd_streams, bwd_streams, masks, tri):
    b = fwd_streams[0].shape[0]
    spec_f = pl.BlockSpec((BPS, CHUNK, RWKV_WIDTH), lambda i, s: (i, s, 0))
    spec_b = pl.BlockSpec((BPS, CHUNK, RWKV_WIDTH), lambda i, s: (i, _bwd_chunk(s), 0))
    shape = jax.ShapeDtypeStruct((b, ALL_LEN, RWKV_WIDTH), F32)
    consts = (masks, tri)
    return pl.pallas_call(
        _rwkv_kernel,
        grid=(b // BPS, N_CHUNKS),
        in_specs=[spec_f] * len(fwd_streams) + [spec_b] * len(bwd_streams) + [_const_spec(p.shape) for p in consts],
        out_specs=[spec_f, spec_b],
        out_shape=[shape] * 2,
        scratch_shapes=[pltpu.VMEM((N_INST, HEAD_DIM, GW), F32)],
        compiler_params=pltpu.CompilerParams(
            dimension_semantics=("arbitrary", "arbitrary"), vmem_limit_bytes=VMEM_LIMIT),
        name="rwkv",
    )(*fwd_streams, *bwd_streams, *consts)


def _attn_kernel(q_ref, k_ref, vt_ref, lam_ref, g_ref, o_ref):
    q = q_ref[0, pl.ds(pl.multiple_of(CTX_LEN + pl.program_id(2) * TQ, TM), TQ)]
    lane = lax.broadcasted_iota(jnp.int32, (1, DIFF_V_DIM), 1)
    first = lane < HEAD_DIM
    zero = jnp.zeros_like(q)
    q2 = jnp.concatenate([jnp.where(first, q, zero), jnp.where(first, zero, q)], axis=0)
    k = k_ref[0]
    vt = vt_ref[...]
    parts = []
    n_tiles = 2 * TQ // Q_COLS
    scores = lambda c: lax.dot_general(k, q2[c * Q_COLS:(c + 1) * Q_COLS], (((1,), (1,)), ((), ())),
                                       preferred_element_type=F32)
    st_next = scores(0)
    for c in range(n_tiles):
        st = st_next
        if c + 1 < n_tiles:
            st_next = scores(c + 1)
        e = jnp.exp2(st - jnp.max(st, axis=0, keepdims=True))
        inv_l = 1.0 / jnp.sum(e, axis=0, keepdims=True)
        parts.append(jnp.dot(vt, e.astype(BF16), preferred_element_type=F32) * inv_l)
    ot = jnp.concatenate(parts, axis=1)
    lp = lam_ref[...]
    lane_sum = lambda t: jnp.sum(t, axis=-1, keepdims=True)
    lam = jnp.exp(lane_sum(lp[0:1] * lp[1:2])) - jnp.exp(lane_sum(lp[2:3] * lp[3:4])) + LAM_INIT
    o = ot[:, :TQ] - lam * ot[:, TQ:]
    o = o * lax.rsqrt(jnp.mean(o * o, axis=0, keepdims=True) + RMS_EPS) * g_ref[...] * (1.0 - LAM_INIT)
    o_ref[0] = o.T.astype(BF16)


def _attn(q, k, vt, lam_p, subln_g):
    b = q.shape[0]
    return pl.pallas_call(
        _attn_kernel,
        grid=(b, DIFF_HEADS, SEQ // TQ),
        in_specs=[pl.BlockSpec((1, ALL_LEN, DIFF_V_DIM), lambda i, h, t: (i, 0, h)),
                  pl.BlockSpec((1, ALL_LEN, DIFF_V_DIM), lambda i, h, t: (i, 0, h)),
                  pl.BlockSpec((DIFF_V_DIM, ALL_LEN), lambda i, h, t: (h, i)),
                  pl.BlockSpec((4, HEAD_DIM), lambda i, h, t: (0, 0)),
                  pl.BlockSpec((DIFF_V_DIM, 1), lambda i, h, t: (0, 0))],
        out_specs=pl.BlockSpec((1, TQ, DIFF_V_DIM), lambda i, h, t: (i, t, h)),
        out_shape=jax.ShapeDtypeStruct((b, SEQ, DIFF_WIDTH), BF16),
        compiler_params=pltpu.CompilerParams(
            dimension_semantics=("arbitrary", "arbitrary", "arbitrary"), vmem_limit_bytes=VMEM_LIMIT),
        name="diff_attn",
    )(q, k, vt, lam_p, subln_g)


_COMBINED = (D_MODEL, RWKV_WIDTH, RWKV_WIDTH, RWKV_WIDTH, RWKV_WIDTH, 2 * D_MODEL)


def _merge_kernel(*refs):
    n_comb = len(_COMBINED)
    combined = refs[:n_comb * SUB]
    (mod_ref, yb_ref, avg_ref, ln_g_ref, ln_b_ref, up_a_ref, up_b_ref, w_out_ref,
     post_ref, pre2_ref, post2_ref, w_in2_ref, w_out2_ref, o_ref) = refs[n_comb * SUB:]
    avg = avg_ref[...]
    for s in range(SUB):
        x_ref, y0_ref, y1_ref, bon_ref, g_ref, gate_ref = combined[n_comb * s:n_comb * (s + 1)]
        rows = slice(s * TM, (s + 1) * TM)
        y = y0_ref[0] + y1_ref[0]
        yc = y - _head_sum(y, avg, 2)
        var = _head_sum(yc * yc, avg, 1)
        yn = yc * lax.rsqrt(var + RWKV_GN_EPS) * ln_g_ref[...] + ln_b_ref[...]
        ya = ((yn + bon_ref[0]) * g_ref[0]).astype(BF16)
        gates = gate_ref[0].astype(F32)
        mix = (gates[:, :D_MODEL] * jnp.dot(ya, up_a_ref[...], preferred_element_type=F32)
               + gates[:, D_MODEL:] * jnp.dot(yb_ref[0, rows], up_b_ref[...], preferred_element_type=F32))
        out = jnp.dot(mix.astype(BF16), w_out_ref[...], preferred_element_type=F32)
        x2 = x_ref[0] + mod_ref[0, 5:6] * _rms(out, post_ref[...])
        o_ref[0, rows] = _ffn_core(x2, mod_ref[0, 6:9], pre2_ref[...], post2_ref[...], w_in2_ref, w_out2_ref)


def _merge_ffn2(x1, m3, y0, y1, bon, g, gates, yb, avg, ln_g, ln_b, up_a, up_b, w_out, post_g,
                pre2_g, post2_g, w_in2, w_out2):
    b = x1.shape[0]
    off = CTX_LEN // TM
    assert (SEQ // TM) % SUB == 0
    combined_specs = []
    for s in range(SUB):
        for width in _COMBINED:
            combined_specs.append(pl.BlockSpec((1, TM, width), lambda i, j, s=s: (i, SUB * j + s + off, 0)))
    plain = lambda width: pl.BlockSpec((1, SUB * TM, width), lambda i, j: (i, j, 0))
    return pl.pallas_call(
        _merge_kernel,
        grid=(b, SEQ // (SUB * TM)),
        in_specs=combined_specs + [
            pl.BlockSpec((1, N_MOD, D_MODEL), lambda i, j: (i, 0, 0)),
            plain(DIFF_WIDTH),
            _const_spec((GW, GW)), _const_spec((1, RWKV_WIDTH)),
            _const_spec((1, RWKV_WIDTH)), _const_spec((RWKV_WIDTH, D_MODEL)),
            _const_spec((DIFF_WIDTH, D_MODEL)), _const_spec((D_MODEL, D_MODEL)),
            _const_spec((1, D_MODEL)),
            _const_spec((1, D_MODEL)), _const_spec((1, D_MODEL)),
            _const_spec((D_MODEL, 2 * D_FF)), _const_spec((D_FF, D_MODEL))],
        out_specs=plain(D_MODEL),
        out_shape=jax.ShapeDtypeStruct((b, SEQ, D_MODEL), F32),
        compiler_params=pltpu.CompilerParams(
            dimension_semantics=("arbitrary", "arbitrary"), vmem_limit_bytes=VMEM_LIMIT),
        name="merge_ffn2",
    )(*([x1, y0, y1, bon, g, gates] * SUB), m3, yb, avg, ln_g, ln_b, up_a, up_b, w_out, post_g,
      pre2_g, post2_g, w_in2, w_out2)


def kernel(x, c, ctx, c_ctx, ada_w, ada_b, pre_norm_g, post_norm_g, ffn1_w_in, ffn1_w_out, mix_w_in,
           rwkv_shift_w, rwkv_w0, rwkv_w2, rwkv_a0, rwkv_a2, rwkv_g2, rwkv_k_k, rwkv_k_a, rwkv_r_k,
           rwkv_ln_g, rwkv_ln_b, diff_lambda, diff_subln_g, branch_up_a, branch_up_b, mix_w_out,
           ffn2_w_in, ffn2_w_out):
    assert x.shape[1:] == (SEQ, D_MODEL) and ctx.shape[1:] == (CTX_LEN, D_MODEL)
    assert ada_w.shape[0] == 1, "single-layer stack"
    b = x.shape[0]
    assert b % BPS == 0
    row = lambda t: t.reshape(1, -1)

    pad = (-(b + 1)) % 8
    cond = jnp.concatenate([c, c_ctx[None], jnp.zeros((pad, D_MODEL), F32)], axis=0)
    m3 = _adaln(cond, ada_w[0], ada_b[0]).reshape(b + 1 + pad, N_MOD, D_MODEL)

    x1 = _ffn1(ctx, x, m3, row(pre_norm_g[0, 0]), row(post_norm_g[0, 0]),
               ffn1_w_in[0].astype(BF16), ffn1_w_out[0].astype(BF16))

    zeros = jnp.zeros((LORA_W // 2, RWKV_WIDTH), F32)
    pad_dir = lambda w: jnp.stack([jnp.concatenate([w[0], zeros]), jnp.concatenate([zeros, w[1]])]).astype(BF16)
    masks, tri, block = _rwkv_masks()
    outs = _mixin(x1, m3, row(pre_norm_g[0, 1]), mix_w_in[0].astype(BF16), _rope_tables(),
                  rwkv_shift_w[0], rwkv_w0[0], pad_dir(rwkv_w2[0]), rwkv_a0[0], pad_dir(rwkv_a2[0]),
                  rwkv_g2[0].astype(BF16), row(rwkv_k_k[0]), row(rwkv_k_a[0]), row(rwkv_r_k[0]), block)
    vt = outs[2]
    per_batch = lambda t: t.reshape(b, ALL_LEN, t.shape[-1])
    q, k, gates, r, v, kk, a_f, a_b, kd_f, kd_b, logw_f, logw_b, bon, g = map(per_batch, outs[:2] + outs[3:])

    y0, y1 = _rwkv((r, v, kk, a_f, kd_f, logw_f), (r, v, kk, a_b, kd_b, logw_b), masks, tri)

    yb = _attn(q, k, vt, diff_lambda[0], diff_subln_g[0].reshape(DIFF_V_DIM, 1))

    avg = (block.astype(F32) / HEAD_DIM).astype(BF16)
    return _merge_ffn2(per_batch(x1), m3, y0, y1, bon, g, gates, yb, avg, row(rwkv_ln_g[0]), row(rwkv_ln_b[0]),
                       branch_up_a[0].astype(BF16), branch_up_b[0].astype(BF16), mix_w_out[0].astype(BF16),
                       row(post_norm_g[0, 1]), row(pre_norm_g[0, 2]), row(post_norm_g[0, 2]),
                       ffn2_w_in[0].astype(BF16), ffn2_w_out[0].astype(BF16))
```

```python
import math

import numpy as np
import jax
import jax.numpy as jnp
from jax import lax
from jax.experimental import pallas as pl
from jax.experimental.pallas import tpu as pltpu

D_MODEL = 1024
SEQ = 2048
CTX_LEN = 256
ALL_LEN = CTX_LEN + SEQ
GRID_W = 64
N_MOD = 9
D_FF = 2816
RMS_EPS = 1e-6

RWKV_HEADS = 8
HEAD_DIM = 64
RWKV_WIDTH = 512
LORA_W = 128
RWKV_COLS = 1920
RWKV_GN_EPS = 64e-5

DIFF_HEADS = 4
DIFF_WIDTH = 512
DIFF_V_DIM = 128
ROPE_BASE = 10000.0
ROPE_FREQS = 16
LAM_INIT = 0.8 - 0.6 * math.exp(-0.0)

MIX_COLS = RWKV_COLS + 3 * DIFF_WIDTH + 2 * D_MODEL

TM = 256
TILES = ALL_LEN // TM
SUB = 2
CHUNK = 64
GROUP = 4
GW = GROUP * HEAD_DIM
N_CHUNKS = ALL_LEN // CHUNK
CTX_CHUNKS = CTX_LEN // CHUNK
HALO = 8
TQ = 1024
Q_COLS = 512
LOG2_E = math.log2(math.e)

F32 = jnp.float32
BF16 = jnp.bfloat16
VMEM_LIMIT = 56 * 1024 * 1024


def _const_spec(shape):
    return pl.BlockSpec(shape, lambda *_: (0,) * len(shape), pipeline_mode=pl.Buffered(1))


def _sigmoid(x):
    return 1.0 / (1.0 + jnp.exp(-x))


def _rms(x, g):
    return x * lax.rsqrt(jnp.mean(x * x, axis=-1, keepdims=True) + RMS_EPS) * g


def _dot(a, b):
    return jnp.dot(a.astype(BF16), b.astype(BF16), preferred_element_type=F32)


def _split_dot(x, m, terms, m_left=False):
    acc = None
    rem = x
    for _ in range(terms):
        part = rem.astype(BF16)
        d = jnp.dot(*((m, part) if m_left else (part, m)), preferred_element_type=F32)
        acc = d if acc is None else acc + d
        rem = rem - part.astype(F32)
    return acc


def _ada_kernel(cond_ref, w_ref, b_ref, o_ref):
    c = cond_ref[...]
    o_ref[...] = _dot(c * _sigmoid(c), w_ref[...]) + b_ref[...]


def _adaln(cond, w, b):
    rows = cond.shape[0]
    n = w.shape[1]
    tn = 2304
    return pl.pallas_call(
        _ada_kernel,
        grid=(n // tn,),
        in_specs=[pl.BlockSpec((rows, D_MODEL), lambda i: (0, 0)),
                  pl.BlockSpec((D_MODEL, tn), lambda i: (0, i)),
                  pl.BlockSpec((1, tn), lambda i: (0, i))],
        out_specs=pl.BlockSpec((rows, tn), lambda i: (0, i)),
        out_shape=jax.ShapeDtypeStruct((rows, n), F32),
        name="adaln",
    )(cond, w, b.reshape(1, n))


def _ffn_core(x, mod, pre_g, post_g, w_in_ref, w_out_ref):
    h = (_rms(x, pre_g) * (1.0 + mod[1:2]) + mod[0:1]).astype(BF16)
    gu = jnp.dot(h, w_in_ref[...], preferred_element_type=F32)
    gate, up = gu[:, :D_FF], gu[:, D_FF:]
    act = (gate * _sigmoid(gate) * up).astype(BF16)
    o = jnp.dot(act, w_out_ref[...], preferred_element_type=F32)
    return x + 0.5 * mod[2:3] * _rms(o, post_g)


def _mod_row(tile, batch):
    return jnp.where(tile % TILES == 0, batch, tile // TILES)


def _ffn1_kernel(*refs):
    pre_ref, post_ref, w_in_ref, w_out_ref, o_ref = refs[3 * SUB:]
    for s in range(SUB):
        ctx_ref, x_ref, mod_ref = refs[3 * s:3 * s + 3]
        tile = SUB * pl.program_id(0) + s
        is_ctx = tile % TILES == 0
        x = jnp.where(is_ctx, ctx_ref[...], x_ref[...])
        o_ref[s * TM:(s + 1) * TM] = _ffn_core(x, mod_ref[0, 0:3], pre_ref[...], post_ref[...],
                                               w_in_ref, w_out_ref)


def _ffn1(ctx, x, mods, pre_g, post_g, w_in, w_out):
    b = x.shape[0]
    lat_tiles = SEQ // TM
    n_tiles = b * TILES
    assert CTX_LEN == TM and n_tiles % SUB == 0
    specs = []
    for s in range(SUB):
        tile = lambda n, s=s: SUB * n + s
        specs += [pl.BlockSpec((TM, D_MODEL), lambda n, t=tile: (t(n) // TILES, 0)),
                  pl.BlockSpec((TM, D_MODEL),
                               lambda n, t=tile: (t(n) // TILES * lat_tiles + jnp.maximum(t(n) % TILES - 1, 0), 0)),
                  pl.BlockSpec((1, N_MOD, D_MODEL), lambda n, t=tile: (_mod_row(t(n), b), 0, 0))]
    out = pl.pallas_call(
        _ffn1_kernel,
        grid=(n_tiles // SUB,),
        in_specs=specs + [_const_spec((1, D_MODEL)), _const_spec((1, D_MODEL)),
                          _const_spec((D_MODEL, 2 * D_FF)), _const_spec((D_FF, D_MODEL))],
        out_specs=pl.BlockSpec((SUB * TM, D_MODEL), lambda n: (n, 0)),
        out_shape=jax.ShapeDtypeStruct((b * ALL_LEN, D_MODEL), F32),
        compiler_params=pltpu.CompilerParams(
            dimension_semantics=("arbitrary",), vmem_limit_bytes=VMEM_LIMIT),
        name="ffn1",
    )(*([ctx.reshape(b * CTX_LEN, D_MODEL), x.reshape(b * SEQ, D_MODEL), mods] * SUB), pre_g, post_g, w_in, w_out)
    return out


def _lane_tile(t, reps):
    return jnp.concatenate([t] * reps, axis=1)


def _head_sum(x, block, terms):
    return jnp.concatenate([_split_dot(x[:, g * GW:(g + 1) * GW], block, terms) for g in range(N_GROUPS)], axis=1)


def _mixin_kernel(x_ref, xp_ref, xn_ref, mod_a_ref, mod_b_ref, rope_a_ref, rope_b_ref, pre_ref, w_ref,
                  sw_ref, w0_ref, w2_ref, a0_ref, a2_ref, g2_ref, k_k_ref, k_a_ref, r_k_ref, block_ref,
                  q_ref, k_ref, vt_ref, gate_ref, r_ref, v_ref, kk_ref, a_f_ref, a_b_ref, kd_f_ref, kd_b_ref,
                  logw_f_ref, logw_b_ref, bon_ref, g_ref):
    pre = pre_ref[...]
    block = block_ref[...]
    first_latent = CTX_LEN // TM
    reps = DIFF_WIDTH // 128
    halos = ((xp_ref[...], x_ref[TM:TM + HALO]), (x_ref[TM - HALO:TM], xn_ref[...]))
    for s, (mod_ref, rope_ref) in enumerate(((mod_a_ref, rope_a_ref), (mod_b_ref, rope_b_ref))):
        tile = slice(s * TM, (s + 1) * TM)
        pos = (SUB * pl.program_id(0) + s) % TILES
        xs = jnp.concatenate([halos[s][0], x_ref[tile], halos[s][1]], axis=0)
        h = (_rms(xs, pre) * (1.0 + mod_ref[0, 4:5]) + mod_ref[0, 3:4]).astype(BF16)
        u_rwkv = jnp.dot(h, w_ref[:, :RWKV_COLS], preferred_element_type=F32)
        u = jnp.dot(h[HALO:HALO + TM], w_ref[:, RWKV_COLS:], preferred_element_type=F32)
        cos_t, sin_a, sin_b = (_lane_tile(rope_ref[i], reps) for i in range(3))

        def rope(z):
            return (z * cos_t + pltpu.roll(z, DIFF_WIDTH - ROPE_FREQS, 1) * sin_a
                    + pltpu.roll(z, ROPE_FREQS, 1) * sin_b)

        q_ref[tile] = (rope(u[:, :DIFF_WIDTH]) * (HEAD_DIM ** -0.5 * LOG2_E)).astype(BF16)
        k_ref[tile] = rope(u[:, DIFF_WIDTH:2 * DIFF_WIDTH]).astype(BF16)
        vt_ref[:, tile] = u[:, 2 * DIFF_WIDTH:3 * DIFF_WIDTH].T.astype(BF16)
        gate_ref[tile] = _sigmoid(u[:, 3 * DIFF_WIDTH:]).astype(BF16)

        ur = u_rwkv[HALO:HALO + TM]
        rows = lax.broadcasted_iota(jnp.int32, (TM, 1), 0)
        has_prev = jnp.logical_and(pos != 0, pos != first_latent)
        has_next = jnp.logical_and(pos != first_latent - 1, pos != TILES - 1)
        prev_row = jnp.where(has_prev, u_rwkv[HALO - 1:HALO], 0.0)
        next_row = jnp.where(has_next, u_rwkv[HALO + TM:HALO + TM + 1], 0.0)
        u_m1 = jnp.where(rows == 0, prev_row, pltpu.roll(ur, 1, 0))
        u_p1 = jnp.where(rows == TM - 1, next_row, pltpu.roll(ur, TM - 1, 0))
        uc = u_m1 * sw_ref[0:1] + ur * sw_ref[1:2] + u_p1 * sw_ref[2:3]

        wdt = RWKV_WIDTH
        r = uc[:, 0:wdt]
        k = uc[:, wdt:2 * wdt]
        v = uc[:, 2 * wdt:3 * wdt]
        l0 = 3 * wdt
        kk = k * k_k_ref[...]
        kk = kk * lax.rsqrt(_head_sum(kk * kk, block, 1) + 1e-12)
        decay_in = jnp.tanh(uc[:, l0:l0 + LORA_W])
        rate_in = uc[:, l0 + LORA_W:l0 + 2 * LORA_W]
        kd_sum = None
        for d, (a_ref, kd_ref, logw_ref) in enumerate(((a_f_ref, kd_f_ref, logw_f_ref),
                                                       (a_b_ref, kd_b_ref, logw_b_ref))):
            w_logit = w0_ref[d:d + 1] + _dot(decay_in, w2_ref[d])
            logw_ref[tile] = -math.exp(-0.5) * _sigmoid(w_logit)
            a = _sigmoid(a0_ref[d:d + 1] + _dot(rate_in, a2_ref[d]))
            kd = k * (1.0 + (a - 1.0) * k_a_ref[...])
            a_ref[tile] = a
            kd_ref[tile] = kd
            kd_sum = kd if kd_sum is None else kd_sum + kd
        r_ref[tile] = r
        v_ref[tile] = v.astype(BF16)
        kk_ref[tile] = kk
        bon_ref[tile] = _head_sum(r * r_k_ref[...] * kd_sum, block, 1) * v
        g_ref[tile] = _dot(_sigmoid(uc[:, l0 + 2 * LORA_W:]), g2_ref[...])


_STREAM_OUTS = ("r", "v", "kk", "a_f", "a_b", "kd_f", "kd_b", "logw_f", "logw_b", "bon", "g")


def _mixin(x1, m3, pre_g, w, rope, sw, w0, w2, a0, a2, g2, k_k, k_a, r_k, block):
    n_rows = x1.shape[0]
    b = n_rows // ALL_LEN
    rows_all = SUB * TM
    assert SUB == 2 and (b * TILES) % SUB == 0
    per_block = rows_all // HALO
    halo_blocks = n_rows // HALO
    block_rows = lambda width: pl.BlockSpec((rows_all, width), lambda n: (n, 0))
    params = (sw, w0, w2, a0, a2, g2, k_k, k_a, r_k, block)
    flat = lambda width, dtype: jax.ShapeDtypeStruct((n_rows, width), dtype)
    wd = RWKV_WIDTH
    per_tile = lambda spec_of: [spec_of(lambda n, s=s: SUB * n + s) for s in range(SUB)]
    return pl.pallas_call(
        _mixin_kernel,
        grid=(n_rows // rows_all,),
        in_specs=[block_rows(D_MODEL),
                  pl.BlockSpec((HALO, D_MODEL), lambda n: (jnp.maximum(n * per_block - 1, 0), 0)),
                  pl.BlockSpec((HALO, D_MODEL), lambda n: (jnp.minimum((n + 1) * per_block, halo_blocks - 1), 0))]
        + per_tile(lambda t: pl.BlockSpec((1, N_MOD, D_MODEL), lambda n: (_mod_row(t(n), b), 0, 0)))
        + per_tile(lambda t: pl.BlockSpec((3, TM, 128), lambda n: (0, t(n) % TILES, 0)))
        + [_const_spec((1, D_MODEL)), _const_spec((D_MODEL, MIX_COLS))] + [_const_spec(p.shape) for p in params],
        out_specs=[block_rows(DIFF_WIDTH), block_rows(DIFF_WIDTH),
                   pl.BlockSpec((DIFF_WIDTH, rows_all), lambda n: (0, n)),
                   block_rows(2 * D_MODEL)] + [block_rows(wd)] * len(_STREAM_OUTS),
        out_shape=[flat(DIFF_WIDTH, BF16), flat(DIFF_WIDTH, BF16),
                   jax.ShapeDtypeStruct((DIFF_WIDTH, n_rows), BF16), flat(2 * D_MODEL, BF16)]
                  + [flat(wd, BF16 if name == "v" else F32) for name in _STREAM_OUTS],
        compiler_params=pltpu.CompilerParams(
            dimension_semantics=("arbitrary",), vmem_limit_bytes=VMEM_LIMIT),
        name="mix_in",
    )(x1, x1, x1, m3, m3, rope, rope, pre_g, w, *params)


def _rope_tables():
    rows = SEQ // GRID_W
    row = jnp.repeat(jnp.arange(rows, dtype=F32), GRID_W)
    col = jnp.tile(jnp.arange(GRID_W, dtype=F32), rows)
    freqs = ROPE_BASE ** (-jnp.arange(ROPE_FREQS, dtype=F32) / ROPE_FREQS)
    p = np.arange(128) % HEAD_DIM
    f = p % ROPE_FREQS
    ang = jnp.where((p // 32)[None, :] == 0, row[:, None], col[:, None]) * freqs[f][None, :]
    first = ((p % 32) < ROPE_FREQS)[None, :]
    cos_x, sin_x = jnp.cos(ang), jnp.sin(ang)
    cos_t = jnp.concatenate([jnp.ones((CTX_LEN, 128), F32), cos_x])
    sin_a = jnp.concatenate([jnp.zeros((CTX_LEN, 128), F32), jnp.where(first, -sin_x, 0.0)])
    sin_b = jnp.concatenate([jnp.zeros((CTX_LEN, 128), F32), jnp.where(first, 0.0, sin_x)])
    return jnp.stack([cos_t, sin_a, sin_b])


_M_STRICT, _M_INCL, _M_DIAG, _M_OFF, _M_EYE = range(5)
INV_BLOCK = 8
BPS = 8
CUM_GROUP = 4
N_GROUPS = RWKV_HEADS // GROUP
N_INST = 2 * BPS * N_GROUPS


def _rwkv_masks():
    t = np.arange(CHUNK)[:, None]
    i = np.arange(CHUNK)[None, :]
    cat = lambda m: np.tile(m.astype(np.float32), (1, GROUP))
    packs, tris = [], []
    for direction in range(2):
        strict = (i < t) if direction == 0 else (i > t)
        incl = (i <= t) if direction == 0 else (i >= t)
        diag = (t // INV_BLOCK == i // INV_BLOCK)
        packs.append(np.stack([cat(strict), cat(incl), cat(strict & diag), cat(strict & ~diag), cat(t == i)]))
        tris.append(np.kron(np.eye(CUM_GROUP), incl).astype(np.float32))
    block = (np.arange(GW)[:, None] // HEAD_DIM == np.arange(GW)[None, :] // HEAD_DIM)
    return jnp.asarray(np.stack(packs), F32), jnp.asarray(np.stack(tris), BF16), jnp.asarray(block, BF16)


def _bmm(a, b):
    return jnp.einsum("imk,ikn->imn", a.astype(BF16), b.astype(BF16), preferred_element_type=F32)


def _bmm_nt(a, b):
    return jnp.einsum("imk,ink->imn", a.astype(BF16), b.astype(BF16), preferred_element_type=F32)


_STREAMS = ("r", "v", "kk", "a", "kd", "logw")


def _rwkv_kernel(*refs):
    n = len(_STREAMS)
    dirs = (dict(zip(_STREAMS, refs[:n])), dict(zip(_STREAMS, refs[n:2 * n])))
    masks_ref, tri_ref, y0_ref, y1_ref, state_ref = refs[2 * n:]

    @pl.when(pl.program_id(1) == 0)
    def _():
        state_ref[...] = jnp.zeros_like(state_ref)

    c = CHUNK

    def inst(name, values=None):
        src = values if values is not None else [d[name] for d in dirs]
        return jnp.stack([src[d][e, :, g * GW:(g + 1) * GW].astype(F32)
                          for d in range(2) for e in range(BPS) for g in range(N_GROUPS)])

    per_dir = N_INST // 2
    mask = lambda m_id: jnp.stack([masks_ref[0, m_id]] * per_dir + [masks_ref[1, m_id]] * per_dir)
    own_block = (lax.broadcasted_iota(jnp.int32, (1, GW, GW), 1) // HEAD_DIM
                 == lax.broadcasted_iota(jnp.int32, (1, GW, GW), 2) // HEAD_DIM)
    bd = lambda x: jnp.where(own_block, jnp.concatenate([x.astype(BF16)] * GROUP, axis=1), jnp.zeros((), BF16))
    rows = lambda *parts: jnp.concatenate(parts, axis=1)
    r, v, kk, a, kd, logw = map(inst, _STREAMS)
    cums = [jnp.concatenate([_split_dot(d["logw"][e:e + CUM_GROUP].reshape(CUM_GROUP * c, RWKV_WIDTH), tri_ref[i], 2,
                                        m_left=True).reshape(CUM_GROUP, c, RWKV_WIDTH)
                             for e in range(0, BPS, CUM_GROUP)], axis=0) for i, d in enumerate(dirs)]
    cum_incl = inst(None, cums)
    a_t = kk * jnp.exp(cum_incl - logw)
    r_t = r * jnp.exp(cum_incl)
    e_inv = jnp.exp(-cum_incl)
    b_t = kk * a * e_inv
    k_t = kd * e_inv
    g = _bmm_nt(rows(a_t, r_t), rows(bd(b_t), bd(k_t)))
    g_ab = g[:, :c, :GW]
    l_ak = g[:, :c, GW:] * mask(_M_STRICT)
    m_rb = g[:, c:, :GW] * mask(_M_INCL)
    m_rk = g[:, c:, GW:] * mask(_M_INCL)
    n1 = -(g_ab * mask(_M_DIAG))
    n2 = _bmm(n1, bd(n1))
    t_d = mask(_M_EYE) + n1
    p = _bmm(rows(t_d, n2), bd(n2))
    t_d, n4 = t_d + p[:, :c], p[:, c:]
    t_d = t_d + _bmm(t_d, bd(n4))
    m1 = -_bmm(g_ab * mask(_M_OFF), bd(t_d))
    p = _bmm(rows(t_d, m1), bd(m1))
    q, m2 = t_d + p[:, :c], p[:, c:]
    p = _bmm(rows(q, m2), bd(m2))
    q, m4 = q + p[:, :c], p[:, c:]
    t_inv = q + _bmm(q, bd(m4))
    p = _bmm(rows(l_ak, m_rk), bd(v))
    x, y_v = p[:, :c], p[:, c:]
    state = state_ref[...]
    ar_s = _bmm_nt(rows(a_t, r_t), bd(state))
    z = _bmm(t_inv, bd(ar_s[:, :c] + x))
    y = ar_s[:, c:] + y_v - _bmm(m_rb, bd(z))
    vz = rows(v, -z)
    kb = rows(k_t, b_t)
    upd = jnp.stack([_dot(vz[i].T, kb[i]) for i in range(N_INST)])
    lane_head = lax.broadcasted_iota(jnp.int32, (1, 1, GW), 2) // HEAD_DIM
    own = upd[:, (GROUP - 1) * HEAD_DIM:]
    for h in range(GROUP - 2, -1, -1):
        own = jnp.where(lane_head == h, upd[:, h * HEAD_DIM:(h + 1) * HEAD_DIM], own)
    total = jnp.sum(logw, axis=1, keepdims=True)
    state_ref[...] = (state + own) * jnp.exp(total)
    for d, y_ref in enumerate((y0_ref, y1_ref)):
        for e in range(BPS):
            first = (d * BPS + e) * N_GROUPS
            y_ref[e] = jnp.concatenate([y[first + g] for g in range(N_GROUPS)], axis=1)


def _bwd_chunk(s):
    return jnp.where(s < CTX_CHUNKS, CTX_CHUNKS - 1 - s, N_CHUNKS + CTX_CHUNKS - 1 - s)


def _rwkv(fwd_streams, bwd_streams, masks, tri):
    b = fwd_streams[0].shape[0]
    spec_f = pl.BlockSpec((BPS, CHUNK, RWKV_WIDTH), lambda i, s: (i, s, 0))
    spec_b = pl.BlockSpec((BPS, CHUNK, RWKV_WIDTH), lambda i, s: (i, _bwd_chunk(s), 0))
    shape = jax.ShapeDtypeStruct((b, ALL_LEN, RWKV_WIDTH), F32)
    consts = (masks, tri)
    return pl.pallas_call(
        _rwkv_kernel,
        grid=(b // BPS, N_CHUNKS),
        in_specs=[spec_f] * len(fwd_streams) + [spec_b] * len(bwd_streams) + [_const_spec(p.shape) for p in consts],
        out_specs=[spec_f, spec_b],
        out_shape=[shape] * 2,
        scratch_shapes=[pltpu.VMEM((N_INST, HEAD_DIM, GW), F32)],
        compiler_params=pltpu.CompilerParams(
            dimension_semantics=("arbitrary", "arbitrary"), vmem_limit_bytes=VMEM_LIMIT),
        name="rwkv",
    )(*fwd_streams, *bwd_streams, *consts)


def _attn_kernel(q_ref, k_ref, vt_ref, lam_ref, g_ref, o_ref):
    q = q_ref[0, pl.ds(pl.multiple_of(CTX_LEN + pl.program_id(2) * TQ, TM), TQ)]
    lane = lax.broadcasted_iota(jnp.int32, (1, DIFF_V_DIM), 1)
    first = lane < HEAD_DIM
    zero = jnp.zeros_like(q)
    q2 = jnp.concatenate([jnp.where(first, q, zero), jnp.where(first, zero, q)], axis=0)
    k = k_ref[0]
    vt = vt_ref[...]
    parts = []
    n_tiles = 2 * TQ // Q_COLS
    scores = lambda c: lax.dot_general(k, q2[c * Q_COLS:(c + 1) * Q_COLS], (((1,), (1,)), ((), ())),
                                       preferred_element_type=F32)
    st_next = scores(0)
    for c in range(n_tiles):
        st = st_next
        if c + 1 < n_tiles:
            st_next = scores(c + 1)
        e = jnp.exp2(st - jnp.max(st, axis=0, keepdims=True))
        inv_l = 1.0 / jnp.sum(e, axis=0, keepdims=True)
        parts.append(jnp.dot(vt, e.astype(BF16), preferred_element_type=F32) * inv_l)
    ot = jnp.concatenate(parts, axis=1)
    lp = lam_ref[...]
    lane_sum = lambda t: jnp.sum(t, axis=-1, keepdims=True)
    lam = jnp.exp(lane_sum(lp[0:1] * lp[1:2])) - jnp.exp(lane_sum(lp[2:3] * lp[3:4])) + LAM_INIT
    o = ot[:, :TQ] - lam * ot[:, TQ:]
    o = o * lax.rsqrt(jnp.mean(o * o, axis=0, keepdims=True) + RMS_EPS) * g_ref[...] * (1.0 - LAM_INIT)
    o_ref[0] = o.T.astype(BF16)


def _attn(q, k, vt, lam_p, subln_g):
    b = q.shape[0]
    return pl.pallas_call(
        _attn_kernel,
        grid=(b, DIFF_HEADS, SEQ // TQ),
        in_specs=[pl.BlockSpec((1, ALL_LEN, DIFF_V_DIM), lambda i, h, t: (i, 0, h)),
                  pl.BlockSpec((1, ALL_LEN, DIFF_V_DIM), lambda i, h, t: (i, 0, h)),
                  pl.BlockSpec((DIFF_V_DIM, ALL_LEN), lambda i, h, t: (h, i)),
                  pl.BlockSpec((4, HEAD_DIM), lambda i, h, t: (0, 0)),
                  pl.BlockSpec((DIFF_V_DIM, 1), lambda i, h, t: (0, 0))],
        out_specs=pl.BlockSpec((1, TQ, DIFF_V_DIM), lambda i, h, t: (i, t, h)),
        out_shape=jax.ShapeDtypeStruct((b, SEQ, DIFF_WIDTH), BF16),
        compiler_params=pltpu.CompilerParams(
            dimension_semantics=("arbitrary", "arbitrary", "arbitrary"), vmem_limit_bytes=VMEM_LIMIT),
        name="diff_attn",
    )(q, k, vt, lam_p, subln_g)


_COMBINED = (D_MODEL, RWKV_WIDTH, RWKV_WIDTH, RWKV_WIDTH, RWKV_WIDTH, 2 * D_MODEL)


def _merge_kernel(*refs):
    n_comb = len(_COMBINED)
    combined = refs[:n_comb * SUB]
    (mod_ref, yb_ref, avg_ref, ln_g_ref, ln_b_ref, up_a_ref, up_b_ref, w_out_ref,
     post_ref, pre2_ref, post2_ref, w_in2_ref, w_out2_ref, o_ref) = refs[n_comb * SUB:]
    avg = avg_ref[...]
    for s in range(SUB):
        x_ref, y0_ref, y1_ref, bon_ref, g_ref, gate_ref = combined[n_comb * s:n_comb * (s + 1)]
        rows = slice(s * TM, (s + 1) * TM)
        y = y0_ref[0] + y1_ref[0]
        yc = y - _head_sum(y, avg, 2)
        var = _head_sum(yc * yc, avg, 1)
        yn = yc * lax.rsqrt(var + RWKV_GN_EPS) * ln_g_ref[...] + ln_b_ref[...]
        ya = ((yn + bon_ref[0]) * g_ref[0]).astype(BF16)
        gates = gate_ref[0].astype(F32)
        mix = (gates[:, :D_MODEL] * jnp.dot(ya, up_a_ref[...], preferred_element_type=F32)
               + gates[:, D_MODEL:] * jnp.dot(yb_ref[0, rows], up_b_ref[...], preferred_element_type=F32))
        out = jnp.dot(mix.astype(BF16), w_out_ref[...], preferred_element_type=F32)
        x2 = x_ref[0] + mod_ref[0, 5:6] * _rms(out, post_ref[...])
        o_ref[0, rows] = _ffn_core(x2, mod_ref[0, 6:9], pre2_ref[...], post2_ref[...], w_in2_ref, w_out2_ref)


def _merge_ffn2(x1, m3, y0, y1, bon, g, gates, yb, avg, ln_g, ln_b, up_a, up_b, w_out, post_g,
                pre2_g, post2_g, w_in2, w_out2):
    b = x1.shape[0]
    off = CTX_LEN // TM
    assert (SEQ // TM) % SUB == 0
    combined_specs = []
    for s in range(SUB):
        for width in _COMBINED:
            combined_specs.append(pl.BlockSpec((1, TM, width), lambda i, j, s=s: (i, SUB * j + s + off, 0)))
    plain = lambda width: pl.BlockSpec((1, SUB * TM, width), lambda i, j: (i, j, 0))
    return pl.pallas_call(
        _merge_kernel,
        grid=(b, SEQ // (SUB * TM)),
        in_specs=combined_specs + [
            pl.BlockSpec((1, N_MOD, D_MODEL), lambda i, j: (i, 0, 0)),
            plain(DIFF_WIDTH),
            _const_spec((GW, GW)), _const_spec((1, RWKV_WIDTH)),
            _const_spec((1, RWKV_WIDTH)), _const_spec((RWKV_WIDTH, D_MODEL)),
            _const_spec((DIFF_WIDTH, D_MODEL)), _const_spec((D_MODEL, D_MODEL)),
            _const_spec((1, D_MODEL)),
            _const_spec((1, D_MODEL)), _const_spec((1, D_MODEL)),
            _const_spec((D_MODEL, 2 * D_FF)), _const_spec((D_FF, D_MODEL))],
        out_specs=plain(D_MODEL),
        out_shape=jax.ShapeDtypeStruct((b, SEQ, D_MODEL), F32),
        compiler_params=pltpu.CompilerParams(
            dimension_semantics=("arbitrary", "arbitrary"), vmem_limit_bytes=VMEM_LIMIT),
        name="merge_ffn2",
    )(*([x1, y0, y1, bon, g, gates] * SUB), m3, yb, avg, ln_g, ln_b, up_a, up_b, w_out, post_g,
      pre2_g, post2_g, w_in2, w_out2)


def kernel(x, c, ctx, c_ctx, ada_w, ada_b, pre_norm_g, post_norm_g, ffn1_w_in, ffn1_w_out, mix_w_in,
           rwkv_shift_w, rwkv_w0, rwkv_w2, rwkv_a0, rwkv_a2, rwkv_g2, rwkv_k_k, rwkv_k_a, rwkv_r_k,
           rwkv_ln_g, rwkv_ln_b, diff_lambda, diff_subln_g, branch_up_a, branch_up_b, mix_w_out,
           ffn2_w_in, ffn2_w_out):
    assert x.shape[1:] == (SEQ, D_MODEL) and ctx.shape[1:] == (CTX_LEN, D_MODEL)
    assert ada_w.shape[0] == 1, "single-layer stack"
    b = x.shape[0]
    assert b % BPS == 0 and BPS % CUM_GROUP == 0
    row = lambda t: t.reshape(1, -1)

    pad = (-(b + 1)) % 8
    cond = jnp.concatenate([c, c_ctx[None], jnp.zeros((pad, D_MODEL), F32)], axis=0)
    m3 = _adaln(cond, ada_w[0], ada_b[0]).reshape(b + 1 + pad, N_MOD, D_MODEL)

    x1 = _ffn1(ctx, x, m3, row(pre_norm_g[0, 0]), row(post_norm_g[0, 0]),
               ffn1_w_in[0].astype(BF16), ffn1_w_out[0].astype(BF16))

    zeros = jnp.zeros((LORA_W // 2, RWKV_WIDTH), F32)
    pad_dir = lambda w: jnp.stack([jnp.concatenate([w[0], zeros]), jnp.concatenate([zeros, w[1]])]).astype(BF16)
    masks, tri, block = _rwkv_masks()
    outs = _mixin(x1, m3, row(pre_norm_g[0, 1]), mix_w_in[0].astype(BF16), _rope_tables(),
                  rwkv_shift_w[0], rwkv_w0[0], pad_dir(rwkv_w2[0]), rwkv_a0[0], pad_dir(rwkv_a2[0]),
                  rwkv_g2[0].astype(BF16), row(rwkv_k_k[0]), row(rwkv_k_a[0]), row(rwkv_r_k[0]), block)
    vt = outs[2]
    per_batch = lambda t: t.reshape(b, ALL_LEN, t.shape[-1])
    q, k, gates, r, v, kk, a_f, a_b, kd_f, kd_b, logw_f, logw_b, bon, g = map(per_batch, outs[:2] + outs[3:])

    y0, y1 = _rwkv((r, v, kk, a_f, kd_f, logw_f), (r, v, kk, a_b, kd_b, logw_b), masks, tri)

    yb = _attn(q, k, vt, diff_lambda[0], diff_subln_g[0].reshape(DIFF_V_DIM, 1))

    avg = (block.astype(F32) / HEAD_DIM).astype(BF16)
    return _merge_ffn2(per_batch(x1), m3, y0, y1, bon, g, gates, yb, avg, row(rwkv_ln_g[0]), row(rwkv_ln_b[0]),
                       branch_up_a[0].astype(BF16), branch_up_b[0].astype(BF16), mix_w_out[0].astype(BF16),
                       row(post_norm_g[0, 1]), row(pre_norm_g[0, 2]), row(post_norm_g[0, 2]),
                       ffn2_w_in[0].astype(BF16), ffn2_w_out[0].astype(BF16))
```
